```python
import math
import jax, jax.numpy as jnp
from jax import lax
import numpy as np

D_MODEL = 1024
BATCH = 8
SEQ = 8192
DEPTH = 2

CHUNK = 64
N_MEM = 256
D_MIX = D_MODEL
D_GLA = D_MIX // 2
D_SGU = D_MIX - D_GLA
GLA_HEADS = 4
D_QK = D_GLA // 2
GLA_DK = D_QK // GLA_HEADS
GLA_DV = D_GLA // GLA_HEADS
GATE_RANK = 16
GATE_TEMP = 16.0
SGU_GROUPS = 4
SGU_BLOCK = 128
SGU_CH = D_SGU // SGU_GROUPS
XATTN_HEADS = 4
XATTN_DH = D_MODEL // XATTN_HEADS
N_EXPERTS = 16
N_EXPERT_GROUPS = 4
EXPERTS_PER_GROUP = N_EXPERTS // N_EXPERT_GROUPS
TOP_K = 2
D_EXPERT = D_MODEL // 2
DN_ALPHA = (2.0 * DEPTH) ** 0.25
DN_BETA = (8.0 * DEPTH) ** -0.25
LN_EPS = 1e-5
RMS_EPS = 1e-6
D_IN = 2 * D_QK + 2 * D_GLA + GATE_RANK + 2 * D_SGU

kernel_name = "hybrid_gla_gmlp_memxattn_groupmoe_deepnorm"


def layer_norm(x, g, b):
    xf = x.astype(jnp.float32)
    mu = jnp.mean(xf, axis=-1, keepdims=True)
    var = jnp.mean(jnp.square(xf - mu), axis=-1, keepdims=True)
    y = (xf - mu) * lax.rsqrt(var + LN_EPS)
    return (y * g.astype(jnp.float32) + b.astype(jnp.float32)).astype(x.dtype)


def split_proj(p):
    sizes = (D_QK, D_QK, D_GLA, D_GLA, GATE_RANK, D_SGU, D_SGU)
    outs, start = [], 0
    for s in sizes:
        outs.append(p[..., start:start + s])
        start += s
    return outs


def gla_group(q, k, v, r, a_low, w_a2, b_a, gn_g):
    B, S, _ = q.shape
    nc = S // CHUNK
    log_a = jax.nn.log_sigmoid((a_low @ w_a2 + b_a).astype(jnp.float32)) / GATE_TEMP

    def to_chunks(t, d):
        return t.astype(jnp.float32).reshape(B, nc, CHUNK, GLA_HEADS, d).transpose(1, 0, 3, 2, 4)

    qc = to_chunks(q, GLA_DK) * (GLA_DK ** -0.5)
    kc = to_chunks(k, GLA_DK)
    vc = to_chunks(v, GLA_DV)
    gc = to_chunks(log_a, GLA_DK)

    def step(state, inp):
        q_c, k_c, v_c, g_c = inp
        bcum = jnp.cumsum(g_c, axis=2)
        b_last = bcum[:, :, -1:, :]
        k_dec = k_c * jnp.exp(b_last - bcum)
        state = jnp.exp(b_last[:, :, 0, :])[..., None] * state + jnp.einsum('bhck,bhcv->bhkv', k_dec, v_c)
        o = jnp.einsum('bhck,bhkv->bhcv', q_c, state)
        return state, o

    s0 = jnp.zeros((B, GLA_HEADS, GLA_DK, GLA_DV), jnp.float32)
    _, o = lax.scan(step, s0, (qc, kc, vc, gc))
    o = o.transpose(1, 0, 3, 2, 4).reshape(B, S, GLA_HEADS, GLA_DV)
    o = o * lax.rsqrt(jnp.mean(jnp.square(o), axis=-1, keepdims=True) + RMS_EPS)
    o = o.reshape(B, S, D_GLA) * gn_g.astype(jnp.float32)
    out = o * jax.nn.silu(r.astype(jnp.float32))
    return out.astype(q.dtype)


def sgu_group(u, v, w_s, b_s, ln_g, ln_b):
    B, S, _ = u.shape
    nb = S // SGU_BLOCK
    u = jax.nn.gelu(u)
    v = jax.nn.gelu(v).reshape(B, S, SGU_GROUPS, SGU_CH)
    v = layer_norm(v, ln_g.reshape(SGU_GROUPS, SGU_CH), ln_b.reshape(SGU_GROUPS, SGU_CH))
    v = v.reshape(B, nb, SGU_BLOCK, SGU_GROUPS, SGU_CH)
    pos = jnp.arange(SGU_BLOCK)
    mask = (pos[None, :] // CHUNK) <= (pos[:, None] // CHUNK)
    w_m = jnp.where(mask[None], w_s, jnp.zeros_like(w_s))
    mixed = jnp.einsum('gts,bnsgc->bntgc', w_m, v) + b_s.T[None, None, :, :, None]
    out = u.reshape(B, nb, SGU_BLOCK, SGU_GROUPS, SGU_CH) * mixed
    return out.reshape(B, S, D_SGU)


def mem_xattn(x, mem, wq, wk, wv, wo):
    B, S, _ = x.shape
    M = mem.shape[1]
    q = (x @ wq).reshape(B, S, XATTN_HEADS, XATTN_DH)
    k = (mem @ wk).reshape(B, M, XATTN_HEADS, XATTN_DH)
    v = (mem @ wv).reshape(B, M, XATTN_HEADS, XATTN_DH)
    s = jnp.einsum('bshd,bmhd->bhsm', q, k).astype(jnp.float32) * (XATTN_DH ** -0.5)
    p = jax.nn.softmax(s, axis=-1).astype(v.dtype)
    o = jnp.einsum('bhsm,bmhd->bshd', p, v).reshape(B, S, D_MODEL)
    return o @ wo


def grouped_moe(x, w_router, b_router, w_gate, w_up, w_down):
    B, S, D = x.shape
    xt = x.reshape(B * S, D)
    scores = jax.nn.softmax((xt @ w_router + b_router).astype(jnp.float32), axis=-1)
    grp = scores.reshape(-1, N_EXPERT_GROUPS, EXPERTS_PER_GROUP)
    group_score = jnp.sum(lax.top_k(grp, TOP_K)[0], axis=-1)
    g_sel = jnp.argmax(group_score, axis=-1)
    in_grp = jnp.sum(grp * jax.nn.one_hot(g_sel, N_EXPERT_GROUPS, dtype=jnp.float32)[:, :, None], axis=1)
    top_w, top_i = lax.top_k(in_grp, TOP_K)
    top_w = top_w / jnp.sum(top_w, axis=-1, keepdims=True)
    eid = g_sel[:, None] * EXPERTS_PER_GROUP + top_i
    gate = jnp.sum(jax.nn.one_hot(eid, N_EXPERTS, dtype=jnp.float32) * top_w[..., None], axis=1)
    gate = gate.astype(x.dtype)
    y = jnp.zeros_like(xt)
    for e in range(N_EXPERTS):
        h = jax.nn.silu(xt @ w_gate[e]) * (xt @ w_up[e])
        y = y + gate[:, e:e + 1] * (h @ w_down[e])
    return y.reshape(B, S, D)


def setup_inputs(seed: int = 0) -> dict:
    key = jax.random.key(seed)
    ks = jax.random.split(key, 24)
    f32 = jnp.float32

    def nrm(k, shape, fan_in, scale=1.0):
        return jax.random.normal(k, shape, f32) * (scale * fan_in ** -0.5)

    def gain(k, shape):
        return 1.0 + 0.02 * jax.random.normal(k, shape, f32)

    return {
        "x": jax.random.normal(ks[0], (BATCH, SEQ, D_MODEL), f32),
        "mem": jax.random.normal(ks[1], (BATCH, N_MEM, D_MODEL), f32),
        "w_in": nrm(ks[2], (DEPTH, D_MODEL, D_IN), D_MODEL),
        "w_a2": nrm(ks[3], (DEPTH, GATE_RANK, D_QK), GATE_RANK),
        "b_a": 1.0 + 0.5 * jax.random.normal(ks[4], (DEPTH, D_QK), f32),
        "gla_norm_g": gain(ks[5], (DEPTH, D_GLA)),
        "w_s": nrm(ks[6], (DEPTH, SGU_GROUPS, SGU_BLOCK, SGU_BLOCK), SGU_BLOCK),
        "b_s": gain(ks[7], (DEPTH, SGU_GROUPS, SGU_BLOCK)),
        "sgu_ln_g": gain(ks[8], (DEPTH, D_SGU)),
        "sgu_ln_b": 0.02 * jax.random.normal(ks[9], (DEPTH, D_SGU), f32),
        "w_out": nrm(ks[10], (DEPTH, D_MIX, D_MODEL), D_MIX, DN_BETA),
        "wq_x": nrm(ks[11], (DEPTH, D_MODEL, D_MODEL), D_MODEL),
        "wk_x": nrm(ks[12], (DEPTH, D_MODEL, D_MODEL), D_MODEL),
        "wv_x": nrm(ks[13], (DEPTH, D_MODEL, D_MODEL), D_MODEL),
        "wo_x": nrm(ks[14], (DEPTH, D_MODEL, D_MODEL), D_MODEL, DN_BETA),
        "w_router": nrm(ks[15], (D_MODEL, N_EXPERTS), D_MODEL),
        "b_router": 0.01 * jax.random.normal(ks[16], (N_EXPERTS,), f32),
        "w_gate": nrm(ks[17], (DEPTH, N_EXPERTS, D_MODEL, D_EXPERT), D_MODEL),
        "w_up": nrm(ks[18], (DEPTH, N_EXPERTS, D_MODEL, D_EXPERT), D_MODEL),
        "w_down": nrm(ks[19], (DEPTH, N_EXPERTS, D_EXPERT, D_MODEL), D_EXPERT, DN_BETA),
        "ln_g": gain(ks[20], (DEPTH, 3, D_MODEL)),
        "ln_b": 0.02 * jax.random.normal(ks[21], (DEPTH, 3, D_MODEL), f32),
    }


def reference(x, mem, w_in, w_a2, b_a, gla_norm_g, w_s, b_s, sgu_ln_g, sgu_ln_b, w_out,
              wq_x, wk_x, wv_x, wo_x, w_router, b_router, w_gate, w_up, w_down, ln_g, ln_b):
    for l in range(DEPTH):
        q, k, v, r, a_low, u, v_sgu = split_proj(x @ w_in[l])
        y_gla = gla_group(q, k, v, r, a_low, w_a2[l], b_a[l], gla_norm_g[l])
        y_sgu = sgu_group(u, v_sgu, w_s[l], b_s[l], sgu_ln_g[l], sgu_ln_b[l])
        h = jnp.concatenate([y_gla, y_sgu], axis=-1) @ w_out[l]
        x = layer_norm(DN_ALPHA * x + h, ln_g[l, 0], ln_b[l, 0])
        h = mem_xattn(x, mem, wq_x[l], wk_x[l], wv_x[l], wo_x[l])
        x = layer_norm(DN_ALPHA * x + h, ln_g[l, 1], ln_b[l, 1])
        h = grouped_moe(x, w_router, b_router, w_gate[l], w_up[l], w_down[l])
        x = layer_norm(DN_ALPHA * x + h, ln_g[l, 2], ln_b[l, 2])
    return x
```

```python
import functools

import jax
import jax.numpy as jnp
from jax import lax
from jax.experimental import pallas as pl
from jax.experimental.pallas import tpu as pltpu

F32 = jnp.float32
BF16 = jnp.bfloat16

D_MODEL = 1024
DEPTH = 2
CHUNK = 64
N_MEM = 256
D_GLA = 512
D_SGU = 512
GLA_HEADS = 4
D_QK = 256
GLA_DK = 64
GLA_DV = 128
GATE_RANK = 16
GATE_TEMP = 16.0
SGU_GROUPS = 4
SGU_BLOCK = 128
SGU_CH = 128
XATTN_HEADS = 4
XATTN_DH = 256
N_EXPERTS = 16
N_EXPERT_GROUPS = 4
EXPERTS_PER_GROUP = 4
D_EXPERT = 512
DN_ALPHA = (2.0 * DEPTH) ** 0.25
LN_EPS = 1e-5
RMS_EPS = 1e-6

LANES = 128
A_PAD = LANES
C_Q, C_K, C_V, C_R = 0, 256, 512, 1024
C_U, C_VS, C_A = 1536, 2048, 2560
D_INP = C_A + A_PAD

PAIRS = ((0, 1), (0, 2), (0, 3), (1, 2), (1, 3), (2, 3))
N_BUCKETS = N_EXPERT_GROUPS * len(PAIRS)

TOK_TILE = 512
MOE_TILE = 512
VMEM_LIMIT = 56 * 1024 * 1024

_HI = lax.Precision.HIGHEST


def _layer_norm(x, g, b):
    mu = jnp.mean(x, axis=-1, keepdims=True)
    xc = x - mu
    var = jnp.mean(xc * xc, axis=-1, keepdims=True)
    return xc * lax.rsqrt(var + LN_EPS) * g + b


def _gelu_tanh(x):
    return 0.5 * x * (1.0 + jnp.tanh(0.7978845608028654 * (x + 0.044715 * (x * x * x))))


def _silu(x):
    return x / (1.0 + jnp.exp(-x))


def _mixer_kernel(x_ref, win_ref, wa2_ref, ba_ref, gng_ref, ws_ref, bs_ref, slg_ref, slb_ref,
                  wout_ref, lng_ref, lnb_ref, o_ref, st_ref, p_ref, g_ref, y_ref):
    T = x_ref.shape[1]

    @pl.when(pl.program_id(1) == 0)
    def _():
        st_ref[...] = jnp.zeros_like(st_ref)

    x = x_ref[0]
    p_ref[...] = jnp.dot(x.astype(BF16), win_ref[...], preferred_element_type=F32)

    z = jnp.dot(p_ref[:, C_A:C_A + A_PAD], wa2_ref[...], precision=_HI,
                preferred_element_type=F32) + ba_ref[...]
    g_ref[...] = (jnp.minimum(z, 0.0) - jnp.log1p(jnp.exp(-jnp.abs(z)))) * (1.0 / GATE_TEMP)

    ri = lax.broadcasted_iota(jnp.int32, (CHUNK, CHUNK), 0)
    ci = lax.broadcasted_iota(jnp.int32, (CHUNK, CHUNK), 1)
    tri = (ci <= ri).astype(F32)
    hr = lax.broadcasted_iota(jnp.int32, (D_GLA, D_QK), 0) // GLA_DV
    hc = lax.broadcasted_iota(jnp.int32, (D_GLA, D_QK), 1) // GLA_DK
    head_mask = (hr == hc).astype(F32)

    for c in range(T // CHUNK):
        rows = pl.ds(c * CHUNK, CHUNK)
        bcum = jnp.dot(tri, g_ref[rows, :], precision=_HI, preferred_element_type=F32)
        b_last = bcum[CHUNK - 1:CHUNK, :]
        k_dec = p_ref[rows, C_K:C_K + D_QK] * jnp.exp(b_last - bcum)
        v_c = p_ref[rows, C_V:C_V + D_GLA]
        upd = lax.dot_general(v_c.astype(BF16), k_dec.astype(BF16), (((0,), (0,)), ((), ())),
                              preferred_element_type=F32)
        st = st_ref[...] * jnp.exp(b_last) + upd * head_mask
        st_ref[...] = st
        q_c = p_ref[rows, C_Q:C_Q + D_QK] * (GLA_DK ** -0.5)
        o_c = lax.dot_general(q_c.astype(BF16), st.astype(BF16), (((1,), (1,)), ((), ())),
                              preferred_element_type=F32)
        r_c = p_ref[rows, C_R:C_R + D_GLA]
        for h in range(GLA_HEADS):
            cols = slice(h * GLA_DV, (h + 1) * GLA_DV)
            oh = o_c[:, cols]
            oh = oh * lax.rsqrt(jnp.mean(oh * oh, axis=-1, keepdims=True) + RMS_EPS)
            y_ref[rows, cols] = (oh * gng_ref[:, cols] * _silu(r_c[:, cols])).astype(BF16)

    ti = lax.broadcasted_iota(jnp.int32, (SGU_BLOCK, SGU_BLOCK), 0) // CHUNK
    si = lax.broadcasted_iota(jnp.int32, (SGU_BLOCK, SGU_BLOCK), 1) // CHUNK
    causal = si <= ti
    for gi in range(SGU_GROUPS):
        w_m = jnp.where(causal, ws_ref[gi], 0.0).astype(BF16)
        cu = slice(C_U + gi * SGU_CH, C_U + (gi + 1) * SGU_CH)
        cv = slice(C_VS + gi * SGU_CH, C_VS + (gi + 1) * SGU_CH)
        cg = slice(gi * SGU_CH, (gi + 1) * SGU_CH)
        for n in range(T // SGU_BLOCK):
            rows = pl.ds(n * SGU_BLOCK, SGU_BLOCK)
            u = _gelu_tanh(p_ref[rows, cu])
            vn = _layer_norm(_gelu_tanh(p_ref[rows, cv]), slg_ref[:, cg], slb_ref[:, cg])
            mixed = jnp.dot(w_m, vn.astype(BF16), preferred_element_type=F32) + bs_ref[gi]
            y_ref[rows, D_GLA + gi * SGU_CH:D_GLA + (gi + 1) * SGU_CH] = (u * mixed).astype(BF16)

    h = jnp.dot(y_ref[...], wout_ref[...], preferred_element_type=F32)
    o_ref[0] = _layer_norm(DN_ALPHA * x + h, lng_ref[...], lnb_ref[...])


def _const_spec(shape):
    nd = len(shape)
    return pl.BlockSpec(shape, lambda *_: (0,) * nd)


def _mixer(x, win, wa2, ba, gng, ws, bs_full, slg, slb, wout, lng, lnb):
    B, S, D = x.shape
    T = TOK_TILE
    tile = pl.BlockSpec((1, T, D), lambda b, s: (b, s, 0))
    consts = (win, wa2, ba, gng, ws, bs_full, slg, slb, wout, lng, lnb)
    return pl.pallas_call(
        _mixer_kernel,
        grid=(B, S // T),
        in_specs=[tile] + [_const_spec(c.shape) for c in consts],
        out_specs=tile,
        out_shape=jax.ShapeDtypeStruct(x.shape, F32),
        scratch_shapes=[
            pltpu.VMEM((D_GLA, D_QK), F32),
            pltpu.VMEM((T, D_INP), F32),
            pltpu.VMEM((T, D_QK), F32),
            pltpu.VMEM((T, D), BF16),
        ],
        compiler_params=pltpu.CompilerParams(
            dimension_semantics=("arbitrary", "arbitrary"), vmem_limit_bytes=VMEM_LIMIT),
        name="mixer",
    )(x, *consts)


def _route(logits):
    m = functools.reduce(jnp.maximum, logits)
    e = [jnp.exp(l - m) for l in logits]
    inv = 1.0 / functools.reduce(jnp.add, e)
    sc = [v * inv for v in e]

    def top2(a):
        first = functools.reduce(jnp.maximum, a)
        second = None
        for i in range(len(a)):
            for j in range(i + 1, len(a)):
                mn = jnp.minimum(a[i], a[j])
                second = mn if second is None else jnp.maximum(second, mn)
        return first + second

    gs = [top2(sc[g * EXPERTS_PER_GROUP:(g + 1) * EXPERTS_PER_GROUP]) for g in range(N_EXPERT_GROUPS)]
    best, g_sel = gs[0], jnp.zeros_like(gs[0])
    for g in range(1, N_EXPERT_GROUPS):
        better = gs[g] > best
        g_sel = jnp.where(better, float(g), g_sel)
        best = jnp.where(better, gs[g], best)
    a = []
    for j in range(EXPERTS_PER_GROUP):
        v = jnp.zeros_like(best)
        for g in range(N_EXPERT_GROUPS):
            v = v + jnp.where(g_sel == float(g), sc[g * EXPERTS_PER_GROUP + j], 0.0)
        a.append(v)
    w1, i1 = a[0], jnp.zeros_like(a[0])
    for j in range(1, EXPERTS_PER_GROUP):
        better = a[j] > w1
        i1 = jnp.where(better, float(j), i1)
        w1 = jnp.where(better, a[j], w1)
    w2, i2 = None, None
    for j in range(EXPERTS_PER_GROUP):
        cand = jnp.where(i1 == float(j), -1.0, a[j])
        if w2 is None:
            w2, i2 = cand, jnp.zeros_like(cand)
        else:
            better = cand > w2
            i2 = jnp.where(better, float(j), i2)
            w2 = jnp.where(better, cand, w2)
    tot = w1 + w2
    w1n, w2n = w1 / tot, w2 / tot
    first_is_lo = i1 < i2
    lo = jnp.where(first_is_lo, i1, i2)
    hi = jnp.where(first_is_lo, i2, i1)
    pair = jnp.zeros_like(lo)
    for pi, (pa, pb) in enumerate(PAIRS):
        pair = jnp.where((lo == float(pa)) & (hi == float(pb)), float(pi), pair)
    bucket = g_sel * float(len(PAIRS)) + pair
    g_lo = jnp.where(first_is_lo, w1n, w2n)
    g_hi = jnp.where(first_is_lo, w2n, w1n)
    return bucket, g_lo, g_hi


def _xattn_kernel(x_ref, mem_ref, wq_ref, wk_ref, wv_ref, wo_ref, lng_ref, lnb_ref, wr_ref, br_ref,
                  o_ref, route_ref, k_ref, v_ref, a_ref):
    @pl.when(pl.program_id(1) == 0)
    def _():
        mb = mem_ref[0].astype(BF16)
        k_ref[...] = jnp.dot(mb, wk_ref[...], preferred_element_type=F32).astype(BF16)
        v_ref[...] = jnp.dot(mb, wv_ref[...], preferred_element_type=F32).astype(BF16)

    x = x_ref[0]
    q = jnp.dot(x.astype(BF16), wq_ref[...], preferred_element_type=F32).astype(BF16)
    for h in range(XATTN_HEADS):
        cols = slice(h * XATTN_DH, (h + 1) * XATTN_DH)
        s = lax.dot_general(q[:, cols], k_ref[:, cols], (((1,), (1,)), ((), ())),
                            preferred_element_type=F32) * (XATTN_DH ** -0.5)
        e = jnp.exp(s - jnp.max(s, axis=-1, keepdims=True))
        p = e / jnp.sum(e, axis=-1, keepdims=True)
        a_ref[:, cols] = jnp.dot(p.astype(BF16), v_ref[:, cols],
                                 preferred_element_type=F32).astype(BF16)
    hres = jnp.dot(a_ref[...], wo_ref[...], preferred_element_type=F32)
    x2 = _layer_norm(DN_ALPHA * x + hres, lng_ref[...], lnb_ref[...])
    o_ref[0] = x2

    lt = lax.dot_general(wr_ref[...], x2.astype(BF16), (((1,), (1,)), ((), ())),
                         preferred_element_type=F32) + br_ref[...]
    bucket, g_lo, g_hi = _route([lt[i:i + 1, :] for i in range(N_EXPERTS)])
    route_ref[0, 0:1, :] = bucket
    route_ref[0, 1:2, :] = g_lo
    route_ref[0, 2:3, :] = g_hi
    route_ref[0, 3:8, :] = jnp.zeros((5, bucket.shape[1]), F32)


def _xattn(x, mem, wq, wk, wv, wo, lng, lnb, wr_t, br_col):
    B, S, D = x.shape
    T = TOK_TILE
    nt = S // T
    tile = pl.BlockSpec((1, T, D), lambda b, s: (b, s, 0))
    consts = (wq, wk, wv, wo, lng, lnb, wr_t, br_col)
    return pl.pallas_call(
        _xattn_kernel,
        grid=(B, nt),
        in_specs=[tile, pl.BlockSpec((1, N_MEM, D), lambda b, s: (b, 0, 0))]
        + [_const_spec(c.shape) for c in consts],
        out_specs=[tile, pl.BlockSpec((1, 8, T), lambda b, s: (b * nt + s, 0, 0))],
        out_shape=[jax.ShapeDtypeStruct(x.shape, F32),
                   jax.ShapeDtypeStruct((B * nt, 8, T), F32)],
        scratch_shapes=[
            pltpu.VMEM((N_MEM, D), BF16),
            pltpu.VMEM((N_MEM, D), BF16),
            pltpu.VMEM((T, D), BF16),
        ],
        compiler_params=pltpu.CompilerParams(
            dimension_semantics=("arbitrary", "arbitrary"), vmem_limit_bytes=VMEM_LIMIT),
        name="xattn",
    )(x, mem, *consts)


def _moe_kernel(elo_ref, ehi_ref, valid_ref, src_ref, x_ref, glo_ref, ghi_ref,
                wg_lo, wu_lo, wd_lo, wg_hi, wu_hi, wd_hi, lng_ref, lnb_ref, o_ref):
    del elo_ref, ehi_ref, src_ref

    @pl.when(valid_ref[pl.program_id(0)] == 1)
    def _():
        x = x_ref[...]
        xb = x.astype(BF16)

        def expert(wg, wu, wd):
            hg = jnp.dot(xb, wg[0], preferred_element_type=F32)
            hu = jnp.dot(xb, wu[0], preferred_element_type=F32)
            return jnp.dot((_silu(hg) * hu).astype(BF16), wd[0], preferred_element_type=F32)

        reps = D_MODEL // LANES
        y = jnp.tile(glo_ref[...], (1, reps)) * expert(wg_lo, wu_lo, wd_lo)
        y = y + jnp.tile(ghi_ref[...], (1, reps)) * expert(wg_hi, wu_hi, wd_hi)
        o_ref[...] = _layer_norm(DN_ALPHA * x + y, lng_ref[...], lnb_ref[...])


def _moe(xs, glo, ghi, wg, wu, wd, lng, lnb, tile_elo, tile_ehi, tile_valid, tile_src):
    P, D = xs.shape
    tm = MOE_TILE
    n_tiles = P // tm
    row = lambda w: pl.BlockSpec((tm, w), lambda i, elo, ehi, va, src: (src[i], 0))
    wlo = lambda s: pl.BlockSpec((1,) + s, lambda i, elo, ehi, va, src: (elo[i], 0, 0))
    whi = lambda s: pl.BlockSpec((1,) + s, lambda i, elo, ehi, va, src: (ehi[i], 0, 0))
    cst = lambda s: pl.BlockSpec(s, lambda i, elo, ehi, va, src: (0, 0))
    gu, dn = (D, D_EXPERT), (D_EXPERT, D)
    grid_spec = pltpu.PrefetchScalarGridSpec(
        num_scalar_prefetch=4,
        grid=(n_tiles,),
        in_specs=[row(D), row(LANES), row(LANES),
                  wlo(gu), wlo(gu), wlo(dn), whi(gu), whi(gu), whi(dn),
                  cst(lng.shape), cst(lnb.shape)],
        out_specs=row(D),
    )
    return pl.pallas_call(
        _moe_kernel,
        grid_spec=grid_spec,
        out_shape=jax.ShapeDtypeStruct((P, D), F32),
        compiler_params=pltpu.CompilerParams(
            dimension_semantics=("arbitrary",), vmem_limit_bytes=VMEM_LIMIT),
        name="moe",
    )(tile_elo, tile_ehi, tile_valid, tile_src, xs, glo, ghi, wg, wu, wd, wg, wu, wd, lng, lnb)


def _dispatch_plan(bucket, n_tok):
    tm = MOE_TILE
    n_tiles = n_tok // tm + N_BUCKETS
    order = jnp.argsort(bucket, stable=True).astype(jnp.int32)
    counts = jnp.sum((bucket[:, None] == jnp.arange(N_BUCKETS, dtype=jnp.int32)[None, :]).astype(jnp.int32),
                     axis=0)
    tiles_per = (counts + tm - 1) // tm
    tile_end = jnp.cumsum(tiles_per)
    tile_start = tile_end - tiles_per
    sorted_start = jnp.cumsum(counts) - counts
    n_valid = tile_end[-1]
    ti = jnp.arange(n_tiles, dtype=jnp.int32)
    tile_bucket = jnp.minimum(jnp.searchsorted(tile_end, ti, side="right"), N_BUCKETS - 1).astype(jnp.int32)
    tile_valid = (ti < n_valid).astype(jnp.int32)
    last = jnp.maximum(n_valid - 1, 0).astype(jnp.int32)
    tile_src = jnp.where(tile_valid == 1, ti, last).astype(jnp.int32)
    tile_bucket = jnp.where(tile_valid == 1, tile_bucket, tile_bucket[last])
    pair = tile_bucket % len(PAIRS)
    grp = tile_bucket // len(PAIRS)
    pa = jnp.asarray([p[0] for p in PAIRS], jnp.int32)
    pb = jnp.asarray([p[1] for p in PAIRS], jnp.int32)
    tile_elo = grp * EXPERTS_PER_GROUP + pa[pair]
    tile_ehi = grp * EXPERTS_PER_GROUP + pb[pair]
    j = (ti - tile_start[tile_bucket])[:, None] * tm + jnp.arange(tm, dtype=jnp.int32)[None, :]
    row_ok = (j < counts[tile_bucket][:, None]) & (tile_valid[:, None] == 1)
    src_sorted = jnp.clip(sorted_start[tile_bucket][:, None] + j, 0, n_tok - 1)
    row_tok = jnp.where(row_ok, order[src_sorted], 0).reshape(-1)
    b_sorted = bucket[order]
    pos_sorted = tile_start[b_sorted] * tm + (jnp.arange(n_tok, dtype=jnp.int32) - sorted_start[b_sorted])
    pos = jnp.zeros((n_tok,), jnp.int32).at[order].set(pos_sorted.astype(jnp.int32), unique_indices=True)
    return row_tok, pos, tile_elo.astype(jnp.int32), tile_ehi.astype(jnp.int32), tile_valid, tile_src


def kernel(x, mem, w_in, w_a2, b_a, gla_norm_g, w_s, b_s, sgu_ln_g, sgu_ln_b, w_out, wq_x, wk_x, wv_x,
           wo_x, w_router, b_router, w_gate, w_up, w_down, ln_g, ln_b):
    B, S, D = x.shape
    n_tok = B * S

    segs = (slice(0, 256), slice(256, 512), slice(512, 1024), slice(1024, 1536),
            slice(1552, 2064), slice(2064, 2576))
    w_in_r = jnp.concatenate(
        [w_in[:, :, s] for s in segs]
        + [jnp.pad(w_in[:, :, 1536:1552], ((0, 0), (0, 0), (0, A_PAD - GATE_RANK)))], axis=-1).astype(BF16)
    w_a2_p = jnp.pad(w_a2, ((0, 0), (0, A_PAD - GATE_RANK), (0, 0)))
    bs_full = jnp.broadcast_to(b_s[:, :, :, None], b_s.shape + (SGU_CH,))
    w_out_b, wq_b, wk_b, wv_b, wo_b = (w.astype(BF16) for w in (w_out, wq_x, wk_x, wv_x, wo_x))
    wg_b, wu_b, wd_b = (w.astype(BF16) for w in (w_gate, w_up, w_down))
    wr_t = w_router.T.astype(BF16)
    br_col = jnp.broadcast_to(b_router[:, None], (N_EXPERTS, TOK_TILE))
    row = lambda v: v.reshape(1, -1)

    for l in range(DEPTH):
        x = _mixer(x, w_in_r[l], w_a2_p[l], row(b_a[l]), row(gla_norm_g[l]), w_s[l], bs_full[l],
                   row(sgu_ln_g[l]), row(sgu_ln_b[l]), w_out_b[l], row(ln_g[l, 0]), row(ln_b[l, 0]))
        x, route = _xattn(x, mem, wq_b[l], wk_b[l], wv_b[l], wo_b[l], row(ln_g[l, 1]), row(ln_b[l, 1]),
                          wr_t, br_col)
        bucket = route[:, 0, :].reshape(-1).astype(jnp.int32)
        g_lo = route[:, 1, :].reshape(-1)
        g_hi = route[:, 2, :].reshape(-1)
        row_tok, pos, t_elo, t_ehi, t_valid, t_src = _dispatch_plan(bucket, n_tok)
        xs = x.reshape(n_tok, D)[row_tok]
        glo_s = jnp.broadcast_to(g_lo[row_tok][:, None], (row_tok.shape[0], LANES))
        ghi_s = jnp.broadcast_to(g_hi[row_tok][:, None], (row_tok.shape[0], LANES))
        ys = _moe(xs, glo_s, ghi_s, wg_b[l], wu_b[l], wd_b[l], row(ln_g[l, 2]), row(ln_b[l, 2]),
                  t_elo, t_ehi, t_valid, t_src)
        x = ys[pos].reshape(B, S, D)
    return x
```

```python
import functools

import jax
import jax.numpy as jnp
from jax import lax
from jax.experimental import pallas as pl
from jax.experimental.pallas import tpu as pltpu

F32 = jnp.float32
BF16 = jnp.bfloat16

D_MODEL = 1024
DEPTH = 2
CHUNK = 64
N_MEM = 256
D_GLA = 512
D_SGU = 512
GLA_HEADS = 4
D_QK = 256
GLA_DK = 64
GLA_DV = 128
GATE_RANK = 16
GATE_TEMP = 16.0
SGU_GROUPS = 4
SGU_BLOCK = 128
SGU_CH = 128
XATTN_HEADS = 4
XATTN_DH = 256
N_EXPERTS = 16
N_EXPERT_GROUPS = 4
EXPERTS_PER_GROUP = 4
D_EXPERT = 512
DN_ALPHA = (2.0 * DEPTH) ** 0.25
LN_EPS = 1e-5
RMS_EPS = 1e-6

LANES = 128
A_PAD = LANES
C_Q, C_K, C_V, C_R = 0, 256, 512, 1024
C_U, C_VS, C_A = 1536, 2048, 2560
D_INP = C_A + A_PAD
D_EXT = D_MODEL + LANES
COL_GLO, COL_GHI = D_MODEL + 1, D_MODEL + 2

PAIRS = ((0, 1), (0, 2), (0, 3), (1, 2), (1, 3), (2, 3))
N_BUCKETS = N_EXPERT_GROUPS * len(PAIRS)

TOK_TILE = 512
MOE_TILE = 512
VMEM_LIMIT = 56 * 1024 * 1024

_HI = lax.Precision.HIGHEST


def _layer_norm(x, g, b):
    mu = jnp.mean(x, axis=-1, keepdims=True)
    xc = x - mu
    var = jnp.mean(xc * xc, axis=-1, keepdims=True)
    return xc * lax.rsqrt(var + LN_EPS) * g + b


def _gelu_tanh(x):
    return 0.5 * x * (1.0 + jnp.tanh(0.7978845608028654 * (x + 0.044715 * (x * x * x))))


def _silu(x):
    return x / (1.0 + jnp.exp(-x))


def _mixer_kernel(x_ref, win_ref, wa2_ref, ba_ref, gng_ref, ws_ref, bs_ref, slg_ref, slb_ref,
                  wout_ref, lng_ref, lnb_ref, o_ref, st_ref, p_ref, g_ref, y_ref):
    T = x_ref.shape[1]

    @pl.when(pl.program_id(1) == 0)
    def _():
        st_ref[...] = jnp.zeros_like(st_ref)

    x = x_ref[0]
    p_ref[...] = jnp.dot(x.astype(BF16), win_ref[...], preferred_element_type=F32)

    z = jnp.dot(p_ref[:, C_A:C_A + A_PAD], wa2_ref[...], precision=_HI,
                preferred_element_type=F32) + ba_ref[...]
    g_ref[...] = (jnp.minimum(z, 0.0) - jnp.log1p(jnp.exp(-jnp.abs(z)))) * (1.0 / GATE_TEMP)

    ri = lax.broadcasted_iota(jnp.int32, (CHUNK, CHUNK), 0)
    ci = lax.broadcasted_iota(jnp.int32, (CHUNK, CHUNK), 1)
    tri = (ci <= ri).astype(F32)
    hr = lax.broadcasted_iota(jnp.int32, (D_GLA, D_QK), 0) // GLA_DV
    hc = lax.broadcasted_iota(jnp.int32, (D_GLA, D_QK), 1) // GLA_DK
    head_mask = (hr == hc).astype(F32)

    for c in range(T // CHUNK):
        rows = pl.ds(c * CHUNK, CHUNK)
        bcum = jnp.dot(tri, g_ref[rows, :], precision=_HI, preferred_element_type=F32)
        b_last = bcum[CHUNK - 1:CHUNK, :]
        k_dec = p_ref[rows, C_K:C_K + D_QK] * jnp.exp(b_last - bcum)
        v_c = p_ref[rows, C_V:C_V + D_GLA]
        upd = lax.dot_general(v_c.astype(BF16), k_dec.astype(BF16), (((0,), (0,)), ((), ())),
                              preferred_element_type=F32)
        st = st_ref[...] * jnp.exp(b_last) + upd * head_mask
        st_ref[...] = st
        q_c = p_ref[rows, C_Q:C_Q + D_QK] * (GLA_DK ** -0.5)
        o_c = lax.dot_general(q_c.astype(BF16), st.astype(BF16), (((1,), (1,)), ((), ())),
                              preferred_element_type=F32)
        r_c = p_ref[rows, C_R:C_R + D_GLA]
        for h in range(GLA_HEADS):
            cols = slice(h * GLA_DV, (h + 1) * GLA_DV)
            oh = o_c[:, cols]
            oh = oh * lax.rsqrt(jnp.mean(oh * oh, axis=-1, keepdims=True) + RMS_EPS)
            y_ref[rows, cols] = (oh * gng_ref[:, cols] * _silu(r_c[:, cols])).astype(BF16)

    ti = lax.broadcasted_iota(jnp.int32, (SGU_BLOCK, SGU_BLOCK), 0) // CHUNK
    si = lax.broadcasted_iota(jnp.int32, (SGU_BLOCK, SGU_BLOCK), 1) // CHUNK
    causal = si <= ti
    for gi in range(SGU_GROUPS):
        w_m = jnp.where(causal, ws_ref[gi], 0.0).astype(BF16)
        cu = slice(C_U + gi * SGU_CH, C_U + (gi + 1) * SGU_CH)
        cv = slice(C_VS + gi * SGU_CH, C_VS + (gi + 1) * SGU_CH)
        cg = slice(gi * SGU_CH, (gi + 1) * SGU_CH)
        for n in range(T // SGU_BLOCK):
            rows = pl.ds(n * SGU_BLOCK, SGU_BLOCK)
            u = _gelu_tanh(p_ref[rows, cu])
            vn = _layer_norm(_gelu_tanh(p_ref[rows, cv]), slg_ref[:, cg], slb_ref[:, cg])
            mixed = jnp.dot(w_m, vn.astype(BF16), preferred_element_type=F32) + bs_ref[gi]
            y_ref[rows, D_GLA + gi * SGU_CH:D_GLA + (gi + 1) * SGU_CH] = (u * mixed).astype(BF16)

    h = jnp.dot(y_ref[...], wout_ref[...], preferred_element_type=F32)
    o_ref[0] = _layer_norm(DN_ALPHA * x + h, lng_ref[...], lnb_ref[...])


def _const_spec(shape):
    nd = len(shape)
    return pl.BlockSpec(shape, lambda *_: (0,) * nd)


def _mixer(x, win, wa2, ba, gng, ws, bs_full, slg, slb, wout, lng, lnb):
    B, S, D = x.shape
    T = TOK_TILE
    tile = pl.BlockSpec((1, T, D), lambda b, s: (b, s, 0))
    consts = (win, wa2, ba, gng, ws, bs_full, slg, slb, wout, lng, lnb)
    return pl.pallas_call(
        _mixer_kernel,
        grid=(B, S // T),
        in_specs=[tile] + [_const_spec(c.shape) for c in consts],
        out_specs=tile,
        out_shape=jax.ShapeDtypeStruct(x.shape, F32),
        scratch_shapes=[
            pltpu.VMEM((D_GLA, D_QK), F32),
            pltpu.VMEM((T, D_INP), F32),
            pltpu.VMEM((T, D_QK), F32),
            pltpu.VMEM((T, D), BF16),
        ],
        compiler_params=pltpu.CompilerParams(
            dimension_semantics=("arbitrary", "arbitrary"), vmem_limit_bytes=VMEM_LIMIT),
        name="mixer",
    )(x, *consts)


def _route(logits):
    m = functools.reduce(jnp.maximum, logits)
    e = [jnp.exp(l - m) for l in logits]
    inv = 1.0 / functools.reduce(jnp.add, e)
    sc = [v * inv for v in e]

    def top2(a):
        first = functools.reduce(jnp.maximum, a)
        second = None
        for i in range(len(a)):
            for j in range(i + 1, len(a)):
                mn = jnp.minimum(a[i], a[j])
                second = mn if second is None else jnp.maximum(second, mn)
        return first + second

    gs = [top2(sc[g * EXPERTS_PER_GROUP:(g + 1) * EXPERTS_PER_GROUP]) for g in range(N_EXPERT_GROUPS)]
    best, g_sel = gs[0], jnp.zeros_like(gs[0])
    for g in range(1, N_EXPERT_GROUPS):
        better = gs[g] > best
        g_sel = jnp.where(better, float(g), g_sel)
        best = jnp.where(better, gs[g], best)
    a = []
    for j in range(EXPERTS_PER_GROUP):
        v = jnp.zeros_like(best)
        for g in range(N_EXPERT_GROUPS):
            v = v + jnp.where(g_sel == float(g), sc[g * EXPERTS_PER_GROUP + j], 0.0)
        a.append(v)
    w1, i1 = a[0], jnp.zeros_like(a[0])
    for j in range(1, EXPERTS_PER_GROUP):
        better = a[j] > w1
        i1 = jnp.where(better, float(j), i1)
        w1 = jnp.where(better, a[j], w1)
    w2, i2 = None, None
    for j in range(EXPERTS_PER_GROUP):
        cand = jnp.where(i1 == float(j), -1.0, a[j])
        if w2 is None:
            w2, i2 = cand, jnp.zeros_like(cand)
        else:
            better = cand > w2
            i2 = jnp.where(better, float(j), i2)
            w2 = jnp.where(better, cand, w2)
    tot = w1 + w2
    w1n, w2n = w1 / tot, w2 / tot
    first_is_lo = i1 < i2
    lo = jnp.where(first_is_lo, i1, i2)
    hi = jnp.where(first_is_lo, i2, i1)
    pair = jnp.zeros_like(lo)
    for pi, (pa, pb) in enumerate(PAIRS):
        pair = jnp.where((lo == float(pa)) & (hi == float(pb)), float(pi), pair)
    bucket = g_sel * float(len(PAIRS)) + pair
    g_lo = jnp.where(first_is_lo, w1n, w2n)
    g_hi = jnp.where(first_is_lo, w2n, w1n)
    return bucket, g_lo, g_hi


def _xattn_kernel(x_ref, mem_ref, wq_ref, wk_ref, wv_ref, wo_ref, lng_ref, lnb_ref, wr_ref, br_ref,
                  o_ref, route_ref, k_ref, v_ref, a_ref):
    @pl.when(pl.program_id(1) == 0)
    def _():
        mb = mem_ref[0].astype(BF16)
        k_ref[...] = jnp.dot(mb, wk_ref[...], preferred_element_type=F32).astype(BF16)
        v_ref[...] = jnp.dot(mb, wv_ref[...], preferred_element_type=F32).astype(BF16)

    x = x_ref[0]
    q = jnp.dot(x.astype(BF16), wq_ref[...], preferred_element_type=F32).astype(BF16)
    for h in range(XATTN_HEADS):
        cols = slice(h * XATTN_DH, (h + 1) * XATTN_DH)
        s = lax.dot_general(q[:, cols], k_ref[:, cols], (((1,), (1,)), ((), ())),
                            preferred_element_type=F32) * (XATTN_DH ** -0.5)
        e = jnp.exp(s - jnp.max(s, axis=-1, keepdims=True))
        p = e / jnp.sum(e, axis=-1, keepdims=True)
        a_ref[:, cols] = jnp.dot(p.astype(BF16), v_ref[:, cols],
                                 preferred_element_type=F32).astype(BF16)
    hres = jnp.dot(a_ref[...], wo_ref[...], preferred_element_type=F32)
    x2 = _layer_norm(DN_ALPHA * x + hres, lng_ref[...], lnb_ref[...])
    o_ref[0, :, 0:D_MODEL] = x2

    lt = lax.dot_general(wr_ref[...], x2.astype(BF16), (((1,), (1,)), ((), ())),
                         preferred_element_type=F32) + br_ref[...]
    bucket, g_lo, g_hi = _route([lt[i:i + 1, :] for i in range(N_EXPERTS)])
    route_ref[0, 0:1, :] = bucket
    route_ref[0, 1:2, :] = g_lo
    route_ref[0, 2:3, :] = g_hi
    route_ref[0, 3:8, :] = jnp.zeros((5, bucket.shape[1]), F32)
    rt = jnp.concatenate([bucket, g_lo, g_hi, jnp.zeros((LANES - 3, bucket.shape[1]), F32)], axis=0)
    o_ref[0, :, D_MODEL:D_EXT] = rt.T


def _xattn(x, mem, wq, wk, wv, wo, lng, lnb, wr_t, br_col):
    B, S, D = x.shape
    T = TOK_TILE
    nt = S // T
    tile = pl.BlockSpec((1, T, D), lambda b, s: (b, s, 0))
    consts = (wq, wk, wv, wo, lng, lnb, wr_t, br_col)
    return pl.pallas_call(
        _xattn_kernel,
        grid=(B, nt),
        in_specs=[tile, pl.BlockSpec((1, N_MEM, D), lambda b, s: (b, 0, 0))]
        + [_const_spec(c.shape) for c in consts],
        out_specs=[pl.BlockSpec((1, T, D_EXT), lambda b, s: (b, s, 0)),
                   pl.BlockSpec((1, 8, T), lambda b, s: (b * nt + s, 0, 0))],
        out_shape=[jax.ShapeDtypeStruct((B, S, D_EXT), F32),
                   jax.ShapeDtypeStruct((B * nt, 8, T), F32)],
        scratch_shapes=[
            pltpu.VMEM((N_MEM, D), BF16),
            pltpu.VMEM((N_MEM, D), BF16),
            pltpu.VMEM((T, D), BF16),
        ],
        compiler_params=pltpu.CompilerParams(
            dimension_semantics=("arbitrary", "arbitrary"), vmem_limit_bytes=VMEM_LIMIT),
        name="xattn",
    )(x, mem, *consts)


ROW_UNROLL = 8


def _moe_kernel(elo_ref, ehi_ref, nrows_ref, tok_cur, tok_nxt, x_hbm,
                wg_lo, wu_lo, wd_lo, wg_hi, wu_hi, wd_hi, lng_ref, lnb_ref, o_hbm,
                xbuf, obuf, gsem, ssem):
    del elo_ref, ehi_ref
    i = pl.program_id(0)
    n_tiles = pl.num_programs(0)
    tm = xbuf.shape[1]
    slot = lax.rem(i, 2)
    nrows = nrows_ref[i]
    has_next = jnp.logical_and(i + 1 < n_tiles, nrows_ref[jnp.minimum(i + 1, n_tiles - 1)] > 0)

    def gather_row(tok_ref, s, r, prio):
        t = tok_ref[0, 0, r]
        pltpu.make_async_copy(x_hbm.at[pl.ds(t, 1)], xbuf.at[s, pl.ds(r, 1)], gsem.at[s]).start(prio)

    def start_gather(tok_ref, s):
        def body(j, c):
            for k in range(ROW_UNROLL):
                gather_row(tok_ref, s, j * ROW_UNROLL + k, k % 2)
            return c
        lax.fori_loop(0, tm // ROW_UNROLL, body, 0)

    def scatter_row(s, r, prio):
        t = tok_cur[0, 0, r]
        pltpu.make_async_copy(obuf.at[s, pl.ds(r, 1)], o_hbm.at[pl.ds(t, 1)], ssem.at[s]).start(prio)

    def start_scatter(s, n):
        @pl.when(n == tm)
        def _():
            def body(j, c):
                for k in range(ROW_UNROLL):
                    scatter_row(s, j * ROW_UNROLL + k, k % 2)
                return c
            lax.fori_loop(0, tm // ROW_UNROLL, body, 0)

        @pl.when(n < tm)
        def _():
            def body(r, c):
                scatter_row(s, r, 0)
                return c
            lax.fori_loop(0, n, body, 0)

    def wait_scatter(s, n):
        @pl.when(n == tm)
        def _():
            pltpu.make_async_copy(obuf.at[s], obuf.at[s], ssem.at[s]).wait()

        @pl.when(n < tm)
        def _():
            def body(r, c):
                pltpu.make_async_copy(obuf.at[s, pl.ds(0, 1)], obuf.at[s, pl.ds(0, 1)], ssem.at[s]).wait()
                return c
            lax.fori_loop(0, n, body, 0)

    @pl.when(jnp.logical_and(i == 0, nrows > 0))
    def _():
        start_gather(tok_cur, 0)

    @pl.when(has_next)
    def _():
        start_gather(tok_nxt, 1 - slot)

    @pl.when(nrows > 0)
    def _():
        pltpu.make_async_copy(xbuf.at[slot], xbuf.at[slot], gsem.at[slot]).wait()
        x = xbuf[slot, :, 0:D_MODEL]
        g_lo = xbuf[slot, :, COL_GLO:COL_GLO + 1]
        g_hi = xbuf[slot, :, COL_GHI:COL_GHI + 1]
        xb = x.astype(BF16)

        def expert(wg, wu, wd):
            hg = jnp.dot(xb, wg[0], preferred_element_type=F32)
            hu = jnp.dot(xb, wu[0], preferred_element_type=F32)
            return jnp.dot((_silu(hg) * hu).astype(BF16), wd[0], preferred_element_type=F32)

        y = g_lo * expert(wg_lo, wu_lo, wd_lo)
        y = y + g_hi * expert(wg_hi, wu_hi, wd_hi)
        obuf[slot] = _layer_norm(DN_ALPHA * x + y, lng_ref[...], lnb_ref[...])
        start_scatter(slot, nrows)

        @pl.when(i > 0)
        def _():
            wait_scatter(1 - slot, nrows_ref[jnp.maximum(i - 1, 0)])

        @pl.when(jnp.logical_not(has_next))
        def _():
            wait_scatter(slot, nrows)


def _moe(x2e, row_tok, wg, wu, wd, lng, lnb, tile_elo, tile_ehi, tile_nrows):
    n_tok, de = x2e.shape
    n_tiles, _, tm = row_tok.shape
    D = D_MODEL
    tok_spec = lambda f: pl.BlockSpec((1, 1, tm), f, memory_space=pltpu.SMEM)
    wlo = lambda s: pl.BlockSpec((1,) + s, lambda i, elo, ehi, nr: (elo[i], 0, 0))
    whi = lambda s: pl.BlockSpec((1,) + s, lambda i, elo, ehi, nr: (ehi[i], 0, 0))
    cst = lambda s: pl.BlockSpec(s, lambda i, elo, ehi, nr: (0, 0))
    gu, dn = (D, D_EXPERT), (D_EXPERT, D)
    grid_spec = pltpu.PrefetchScalarGridSpec(
        num_scalar_prefetch=3,
        grid=(n_tiles,),
        in_specs=[tok_spec(lambda i, elo, ehi, nr: (i, 0, 0)),
                  tok_spec(lambda i, elo, ehi, nr: (jnp.minimum(i + 1, n_tiles - 1), 0, 0)),
                  pl.BlockSpec(memory_space=pl.ANY),
                  wlo(gu), wlo(gu), wlo(dn), whi(gu), whi(gu), whi(dn),
                  cst(lng.shape), cst(lnb.shape)],
        out_specs=pl.BlockSpec(memory_space=pl.ANY),
        scratch_shapes=[
            pltpu.VMEM((2, tm, de), F32),
            pltpu.VMEM((2, tm, D), F32),
            pltpu.SemaphoreType.DMA((2,)),
            pltpu.SemaphoreType.DMA((2,)),
        ],
    )
    return pl.pallas_call(
        _moe_kernel,
        grid_spec=grid_spec,
        out_shape=jax.ShapeDtypeStruct((n_tok, D), F32),
        compiler_params=pltpu.CompilerParams(
            dimension_semantics=("arbitrary",), vmem_limit_bytes=VMEM_LIMIT),
        name="moe",
    )(tile_elo, tile_ehi, tile_nrows, row_tok, row_tok, x2e, wg, wu, wd, wg, wu, wd, lng, lnb)


def _dispatch_plan(bucket, n_tok):
    tm = MOE_TILE
    n_tiles = n_tok // tm + N_BUCKETS
    order = jnp.argsort(bucket, stable=True).astype(jnp.int32)
    kb = jnp.arange(N_BUCKETS, dtype=jnp.int32)
    counts = jnp.sum((bucket[:, None] == kb[None, :]).astype(jnp.int32), axis=0)
    tiles_per = (counts + tm - 1) // tm
    tile_end = jnp.cumsum(tiles_per)
    tile_start = tile_end - tiles_per
    sorted_start = jnp.cumsum(counts) - counts
    n_valid = tile_end[-1]
    ti = jnp.arange(n_tiles, dtype=jnp.int32)
    valid = ti < n_valid
    tile_bucket = jnp.sum((ti[:, None] >= tile_end[None, :]).astype(jnp.int32), axis=1)
    last_bucket = jnp.sum((n_valid - 1 >= tile_end).astype(jnp.int32))
    tile_bucket = jnp.minimum(jnp.where(valid, tile_bucket, last_bucket), N_BUCKETS - 1)
    pa = jnp.asarray([p[0] for p in PAIRS], jnp.int32)
    pb = jnp.asarray([p[1] for p in PAIRS], jnp.int32)
    grp, pair = tile_bucket // len(PAIRS), tile_bucket % len(PAIRS)
    tile_elo = grp * EXPERTS_PER_GROUP + pa[pair]
    tile_ehi = grp * EXPERTS_PER_GROUP + pb[pair]
    first_row = (ti - tile_start[tile_bucket]) * tm
    tile_nrows = jnp.where(valid, jnp.clip(counts[tile_bucket] - first_row, 0, tm), 0)
    start = jnp.clip(sorted_start[tile_bucket] + first_row, 0, n_tok)
    order_pad = jnp.concatenate([order, jnp.zeros((tm,), jnp.int32)])
    row_tok = jax.vmap(lambda s: lax.dynamic_slice(order_pad, (s,), (tm,)))(start)
    return (row_tok.reshape(n_tiles, 1, tm), tile_elo.astype(jnp.int32), tile_ehi.astype(jnp.int32),
            tile_nrows.astype(jnp.int32))


def kernel(x, mem, w_in, w_a2, b_a, gla_norm_g, w_s, b_s, sgu_ln_g, sgu_ln_b, w_out, wq_x, wk_x, wv_x,
           wo_x, w_router, b_router, w_gate, w_up, w_down, ln_g, ln_b):
    B, S, D = x.shape
    n_tok = B * S

    segs = (slice(0, 256), slice(256, 512), slice(512, 1024), slice(1024, 1536),
            slice(1552, 2064), slice(2064, 2576))
    w_in_r = jnp.concatenate(
        [w_in[:, :, s] for s in segs]
        + [jnp.pad(w_in[:, :, 1536:1552], ((0, 0), (0, 0), (0, A_PAD - GATE_RANK)))], axis=-1).astype(BF16)
    w_a2_p = jnp.pad(w_a2, ((0, 0), (0, A_PAD - GATE_RANK), (0, 0)))
    bs_full = jnp.broadcast_to(b_s[:, :, :, None], b_s.shape + (SGU_CH,))
    w_out_b, wq_b, wk_b, wv_b, wo_b = (w.astype(BF16) for w in (w_out, wq_x, wk_x, wv_x, wo_x))
    wg_b, wu_b, wd_b = (w.astype(BF16) for w in (w_gate, w_up, w_down))
    wr_t = w_router.T.astype(BF16)
    br_col = jnp.broadcast_to(b_router[:, None], (N_EXPERTS, TOK_TILE))
    row = lambda v: v.reshape(1, -1)

    for l in range(DEPTH):
        x = _mixer(x, w_in_r[l], w_a2_p[l], row(b_a[l]), row(gla_norm_g[l]), w_s[l], bs_full[l],
                   row(sgu_ln_g[l]), row(sgu_ln_b[l]), w_out_b[l], row(ln_g[l, 0]), row(ln_b[l, 0]))
        x2e, route = _xattn(x, mem, wq_b[l], wk_b[l], wv_b[l], wo_b[l], row(ln_g[l, 1]), row(ln_b[l, 1]),
                            wr_t, br_col)
        bucket = route[:, 0, :].reshape(-1).astype(jnp.int32)
        row_tok, t_elo, t_ehi, t_nrows = _dispatch_plan(bucket, n_tok)
        x = _moe(x2e.reshape(n_tok, D_EXT), row_tok, wg_b[l], wu_b[l], wd_b[l],
                 row(ln_g[l, 2]), row(ln_b[l, 2]), t_elo, t_ehi, t_nrows).reshape(B, S, D)
    return x
```

```python
import functools

import jax
import jax.numpy as jnp
from jax import lax
from jax.experimental import pallas as pl
from jax.experimental.pallas import tpu as pltpu

F32 = jnp.float32
BF16 = jnp.bfloat16

D_MODEL = 1024
DEPTH = 2
CHUNK = 64
N_MEM = 256
D_GLA = 512
D_SGU = 512
GLA_HEADS = 4
D_QK = 256
GLA_DK = 64
GLA_DV = 128
GATE_RANK = 16
GATE_TEMP = 16.0
SGU_GROUPS = 4
SGU_BLOCK = 128
SGU_CH = 128
XATTN_HEADS = 4
XATTN_DH = 256
N_EXPERTS = 16
N_EXPERT_GROUPS = 4
EXPERTS_PER_GROUP = 4
D_EXPERT = 512
DN_ALPHA = (2.0 * DEPTH) ** 0.25
LN_EPS = 1e-5
RMS_EPS = 1e-6

LANES = 128
SUBLANES = 8
A_PAD = LANES
C_Q, C_K, C_V, C_R = 0, 256, 512, 1024
C_U, C_VS, C_A = 1536, 2048, 2560
D_INP = C_A + A_PAD
D_EXT = D_MODEL + LANES
COL_GLO, COL_GHI = D_MODEL + 1, D_MODEL + 2

PAIRS = ((0, 1), (0, 2), (0, 3), (1, 2), (1, 3), (2, 3))
N_BUCKETS = N_EXPERT_GROUPS * len(PAIRS)

TOK_TILE = 512
MOE_TILE = 512
VMEM_LIMIT = 56 * 1024 * 1024

_HI = lax.Precision.HIGHEST


def _layer_norm(x, g, b):
    mu = jnp.mean(x, axis=-1, keepdims=True)
    xc = x - mu
    var = jnp.mean(xc * xc, axis=-1, keepdims=True)
    return xc * lax.rsqrt(var + LN_EPS) * g + b


def _gelu_tanh(x):
    return 0.5 * x * (1.0 + jnp.tanh(0.7978845608028654 * (x + 0.044715 * (x * x * x))))


def _silu(x):
    return x / (1.0 + jnp.exp(-x))


def _mixer_kernel(x_ref, win_ref, wa2_ref, ba_ref, gng_ref, ws_ref, bs_ref, slg_ref, slb_ref,
                  wout_ref, lng_ref, lnb_ref, o_ref, st_ref, p_ref, g_ref, y_ref):
    T = x_ref.shape[1]

    @pl.when(pl.program_id(1) == 0)
    def _():
        st_ref[...] = jnp.zeros_like(st_ref)

    x = x_ref[0]
    p_ref[...] = jnp.dot(x.astype(BF16), win_ref[...], preferred_element_type=F32)

    z = jnp.dot(p_ref[:, C_A:C_A + A_PAD], wa2_ref[...], precision=_HI,
                preferred_element_type=F32) + ba_ref[...]
    g_ref[...] = (jnp.minimum(z, 0.0) - jnp.log1p(jnp.exp(-jnp.abs(z)))) * (1.0 / GATE_TEMP)

    ri = lax.broadcasted_iota(jnp.int32, (CHUNK, CHUNK), 0)
    ci = lax.broadcasted_iota(jnp.int32, (CHUNK, CHUNK), 1)
    tri = (ci <= ri).astype(F32)
    hr = lax.broadcasted_iota(jnp.int32, (D_GLA, D_QK), 0) // GLA_DV
    hc = lax.broadcasted_iota(jnp.int32, (D_GLA, D_QK), 1) // GLA_DK
    head_mask = (hr == hc).astype(F32)

    for c in range(T // CHUNK):
        rows = pl.ds(c * CHUNK, CHUNK)
        bcum = jnp.dot(tri, g_ref[rows, :], precision=_HI, preferred_element_type=F32)
        b_last = bcum[CHUNK - 1:CHUNK, :]
        k_dec = p_ref[rows, C_K:C_K + D_QK] * jnp.exp(b_last - bcum)
        v_c = p_ref[rows, C_V:C_V + D_GLA]
        upd = lax.dot_general(v_c.astype(BF16), k_dec.astype(BF16), (((0,), (0,)), ((), ())),
                              preferred_element_type=F32)
        st = st_ref[...] * jnp.exp(b_last) + upd * head_mask
        st_ref[...] = st
        q_c = p_ref[rows, C_Q:C_Q + D_QK] * (GLA_DK ** -0.5)
        o_c = lax.dot_general(q_c.astype(BF16), st.astype(BF16), (((1,), (1,)), ((), ())),
                              preferred_element_type=F32)
        r_c = p_ref[rows, C_R:C_R + D_GLA]
        for h in range(GLA_HEADS):
            cols = slice(h * GLA_DV, (h + 1) * GLA_DV)
            oh = o_c[:, cols]
            oh = oh * lax.rsqrt(jnp.mean(oh * oh, axis=-1, keepdims=True) + RMS_EPS)
            y_ref[rows, cols] = (oh * gng_ref[:, cols] * _silu(r_c[:, cols])).astype(BF16)

    ti = lax.broadcasted_iota(jnp.int32, (SGU_BLOCK, SGU_BLOCK), 0) // CHUNK
    si = lax.broadcasted_iota(jnp.int32, (SGU_BLOCK, SGU_BLOCK), 1) // CHUNK
    causal = si <= ti
    for gi in range(SGU_GROUPS):
        w_m = jnp.where(causal, ws_ref[gi], 0.0).astype(BF16)
        cu = slice(C_U + gi * SGU_CH, C_U + (gi + 1) * SGU_CH)
        cv = slice(C_VS + gi * SGU_CH, C_VS + (gi + 1) * SGU_CH)
        cg = slice(gi * SGU_CH, (gi + 1) * SGU_CH)
        for n in range(T // SGU_BLOCK):
            rows = pl.ds(n * SGU_BLOCK, SGU_BLOCK)
            u = _gelu_tanh(p_ref[rows, cu])
            vn = _layer_norm(_gelu_tanh(p_ref[rows, cv]), slg_ref[:, cg], slb_ref[:, cg])
            mixed = jnp.dot(w_m, vn.astype(BF16), preferred_element_type=F32) + bs_ref[gi]
            y_ref[rows, D_GLA + gi * SGU_CH:D_GLA + (gi + 1) * SGU_CH] = (u * mixed).astype(BF16)

    h = jnp.dot(y_ref[...], wout_ref[...], preferred_element_type=F32)
    o_ref[0] = _layer_norm(DN_ALPHA * x + h, lng_ref[...], lnb_ref[...])


def _const_spec(shape):
    nd = len(shape)
    return pl.BlockSpec(shape, lambda *_: (0,) * nd)


def _mixer(x, win, wa2, ba, gng, ws, bs_full, slg, slb, wout, lng, lnb):
    B, S, D = x.shape
    T = TOK_TILE
    tile = pl.BlockSpec((1, T, D), lambda b, s: (b, s, 0))
    consts = (win, wa2, ba, gng, ws, bs_full, slg, slb, wout, lng, lnb)
    return pl.pallas_call(
        _mixer_kernel,
        grid=(B, S // T),
        in_specs=[tile] + [_const_spec(c.shape) for c in consts],
        out_specs=tile,
        out_shape=jax.ShapeDtypeStruct(x.shape, F32),
        scratch_shapes=[
            pltpu.VMEM((D_GLA, D_QK), F32),
            pltpu.VMEM((T, D_INP), F32),
            pltpu.VMEM((T, D_QK), F32),
            pltpu.VMEM((T, D), BF16),
        ],
        compiler_params=pltpu.CompilerParams(
            dimension_semantics=("arbitrary", "arbitrary"), vmem_limit_bytes=VMEM_LIMIT),
        name="mixer",
    )(x, *consts)


def _route(logits):
    m = functools.reduce(jnp.maximum, logits)
    e = [jnp.exp(l - m) for l in logits]
    inv = 1.0 / functools.reduce(jnp.add, e)
    sc = [v * inv for v in e]

    def top2(a):
        first = functools.reduce(jnp.maximum, a)
        second = None
        for i in range(len(a)):
            for j in range(i + 1, len(a)):
                mn = jnp.minimum(a[i], a[j])
                second = mn if second is None else jnp.maximum(second, mn)
        return first + second

    gs = [top2(sc[g * EXPERTS_PER_GROUP:(g + 1) * EXPERTS_PER_GROUP]) for g in range(N_EXPERT_GROUPS)]
    best, g_sel = gs[0], jnp.zeros_like(gs[0])
    for g in range(1, N_EXPERT_GROUPS):
        better = gs[g] > best
        g_sel = jnp.where(better, float(g), g_sel)
        best = jnp.where(better, gs[g], best)
    a = []
    for j in range(EXPERTS_PER_GROUP):
        v = jnp.zeros_like(best)
        for g in range(N_EXPERT_GROUPS):
            v = v + jnp.where(g_sel == float(g), sc[g * EXPERTS_PER_GROUP + j], 0.0)
        a.append(v)
    w1, i1 = a[0], jnp.zeros_like(a[0])
    for j in range(1, EXPERTS_PER_GROUP):
        better = a[j] > w1
        i1 = jnp.where(better, float(j), i1)
        w1 = jnp.where(better, a[j], w1)
    w2, i2 = None, None
    for j in range(EXPERTS_PER_GROUP):
        cand = jnp.where(i1 == float(j), -1.0, a[j])
        if w2 is None:
            w2, i2 = cand, jnp.zeros_like(cand)
        else:
            better = cand > w2
            i2 = jnp.where(better, float(j), i2)
            w2 = jnp.where(better, cand, w2)
    tot = w1 + w2
    w1n, w2n = w1 / tot, w2 / tot
    first_is_lo = i1 < i2
    lo = jnp.where(first_is_lo, i1, i2)
    hi = jnp.where(first_is_lo, i2, i1)
    pair = jnp.zeros_like(lo)
    for pi, (pa, pb) in enumerate(PAIRS):
        pair = jnp.where((lo == float(pa)) & (hi == float(pb)), float(pi), pair)
    bucket = g_sel * float(len(PAIRS)) + pair
    g_lo = jnp.where(first_is_lo, w1n, w2n)
    g_hi = jnp.where(first_is_lo, w2n, w1n)
    return bucket, g_lo, g_hi


def _xattn_kernel(x_ref, mem_ref, wq_ref, wk_ref, wv_ref, wo_ref, lng_ref, lnb_ref, wr_ref, br_ref,
                  o_ref, route_ref, k_ref, v_ref, a_ref):
    @pl.when(pl.program_id(1) == 0)
    def _():
        mb = mem_ref[0].astype(BF16)
        k_ref[...] = jnp.dot(mb, wk_ref[...], preferred_element_type=F32).astype(BF16)
        v_ref[...] = jnp.dot(mb, wv_ref[...], preferred_element_type=F32).astype(BF16)

    x = x_ref[0]
    q = jnp.dot(x.astype(BF16), wq_ref[...], preferred_element_type=F32).astype(BF16)
    for h in range(XATTN_HEADS):
        cols = slice(h * XATTN_DH, (h + 1) * XATTN_DH)
        s = lax.dot_general(q[:, cols], k_ref[:, cols], (((1,), (1,)), ((), ())),
                            preferred_element_type=F32) * (XATTN_DH ** -0.5)
        e = jnp.exp(s - jnp.max(s, axis=-1, keepdims=True))
        p = e / jnp.sum(e, axis=-1, keepdims=True)
        a_ref[:, cols] = jnp.dot(p.astype(BF16), v_ref[:, cols],
                                 preferred_element_type=F32).astype(BF16)
    hres = jnp.dot(a_ref[...], wo_ref[...], preferred_element_type=F32)
    x2 = _layer_norm(DN_ALPHA * x + hres, lng_ref[...], lnb_ref[...])
    o_ref[0, :, 0:D_MODEL] = x2

    lt = lax.dot_general(wr_ref[...], x2.astype(BF16), (((1,), (1,)), ((), ())),
                         preferred_element_type=F32) + br_ref[...]
    bucket, g_lo, g_hi = _route([lt[i:i + 1, :] for i in range(N_EXPERTS)])
    route_ref[0, 0:1, :] = bucket
    route_ref[0, 1:2, :] = g_lo
    route_ref[0, 2:3, :] = g_hi
    route_ref[0, 3:8, :] = jnp.zeros((5, bucket.shape[1]), F32)
    rt = jnp.concatenate([bucket, g_lo, g_hi, jnp.zeros((LANES - 3, bucket.shape[1]), F32)], axis=0)
    o_ref[0, :, D_MODEL:D_EXT] = rt.T


def _xattn(x, mem, wq, wk, wv, wo, lng, lnb, wr_t, br_col):
    B, S, D = x.shape
    T = TOK_TILE
    nt = S // T
    tile = pl.BlockSpec((1, T, D), lambda b, s: (b, s, 0))
    consts = (wq, wk, wv, wo, lng, lnb, wr_t, br_col)
    return pl.pallas_call(
        _xattn_kernel,
        grid=(B, nt),
        in_specs=[tile, pl.BlockSpec((1, N_MEM, D), lambda b, s: (b, 0, 0))]
        + [_const_spec(c.shape) for c in consts],
        out_specs=[pl.BlockSpec((1, T, D_EXT), lambda b, s: (b, s, 0)),
                   pl.BlockSpec((1, 8, T), lambda b, s: (b * nt + s, 0, 0))],
        out_shape=[jax.ShapeDtypeStruct((B, S, D_EXT), F32),
                   jax.ShapeDtypeStruct((B * nt, 8, T), F32)],
        scratch_shapes=[
            pltpu.VMEM((N_MEM, D), BF16),
            pltpu.VMEM((N_MEM, D), BF16),
            pltpu.VMEM((T, D), BF16),
        ],
        compiler_params=pltpu.CompilerParams(
            dimension_semantics=("arbitrary", "arbitrary"), vmem_limit_bytes=VMEM_LIMIT),
        name="xattn",
    )(x, mem, *consts)


def _moe_kernel(elo_ref, ehi_ref, nrows_ref, start_ref, order_ref, x_hbm,
                wg_lo, wu_lo, wd_lo, wg_hi, wu_hi, wd_hi, lng_ref, lnb_ref, o_hbm,
                xbuf, obuf, gsem, ssem):
    del elo_ref, ehi_ref
    i = pl.program_id(0)
    n_tiles = pl.num_programs(0)
    tm = xbuf.shape[1] * SUBLANES
    slot = lax.rem(i, 2)
    nrows = nrows_ref[i]
    nxt = jnp.minimum(i + 1, n_tiles - 1)
    has_next = jnp.logical_and(i + 1 < n_tiles, nrows_ref[nxt] > 0)

    def start_gather(tile, s):
        base = start_ref[tile]

        def body(j, c):
            for k in range(SUBLANES):
                t = order_ref[base + j * SUBLANES + k]
                pltpu.make_async_copy(x_hbm.at[pl.ds(t, 1)], xbuf.at[s, j, pl.ds(k, 1)],
                                      gsem.at[s]).start(k % 2)
            return c
        lax.fori_loop(0, tm // SUBLANES, body, 0)

    def start_scatter(s, n):
        base = start_ref[i]

        @pl.when(n == tm)
        def _():
            def body(j, c):
                for k in range(SUBLANES):
                    t = order_ref[base + j * SUBLANES + k]
                    pltpu.make_async_copy(obuf.at[s, j, pl.ds(k, 1)], o_hbm.at[pl.ds(t, 1)],
                                          ssem.at[s]).start(k % 2)
                return c
            lax.fori_loop(0, tm // SUBLANES, body, 0)

        @pl.when(n < tm)
        def _():
            def body(r, c):
                t = order_ref[base + r]
                pltpu.make_async_copy(obuf.at[s, r // SUBLANES, pl.ds(r % SUBLANES, 1)],
                                      o_hbm.at[pl.ds(t, 1)], ssem.at[s]).start()
                return c
            lax.fori_loop(0, n, body, 0)

    def wait_scatter(s, n):
        @pl.when(n == tm)
        def _():
            pltpu.make_async_copy(obuf.at[s], obuf.at[s], ssem.at[s]).wait()

        @pl.when(n < tm)
        def _():
            def body(r, c):
                pltpu.make_async_copy(obuf.at[s, 0, pl.ds(0, 1)], obuf.at[s, 0, pl.ds(0, 1)],
                                      ssem.at[s]).wait()
                return c
            lax.fori_loop(0, n, body, 0)

    @pl.when(jnp.logical_and(i == 0, nrows > 0))
    def _():
        start_gather(0, 0)

    @pl.when(has_next)
    def _():
        start_gather(nxt, 1 - slot)

    @pl.when(nrows > 0)
    def _():
        pltpu.make_async_copy(xbuf.at[slot], xbuf.at[slot], gsem.at[slot]).wait()
        xe = xbuf[slot].reshape(tm, xbuf.shape[3])
        x = xe[:, 0:D_MODEL]
        g_lo = xe[:, COL_GLO:COL_GLO + 1]
        g_hi = xe[:, COL_GHI:COL_GHI + 1]
        xb = x.astype(BF16)

        def expert(wg, wu, wd):
            hg = jnp.dot(xb, wg[0], preferred_element_type=F32)
            hu = jnp.dot(xb, wu[0], preferred_element_type=F32)
            return jnp.dot((_silu(hg) * hu).astype(BF16), wd[0], preferred_element_type=F32)

        y = g_lo * expert(wg_lo, wu_lo, wd_lo)
        y = y + g_hi * expert(wg_hi, wu_hi, wd_hi)
        res = _layer_norm(DN_ALPHA * x + y, lng_ref[...], lnb_ref[...])
        obuf[slot] = res.reshape(tm // SUBLANES, SUBLANES, D_MODEL)
        start_scatter(slot, nrows)

        @pl.when(i > 0)
        def _():
            wait_scatter(1 - slot, nrows_ref[jnp.maximum(i - 1, 0)])

        @pl.when(jnp.logical_not(has_next))
        def _():
            wait_scatter(slot, nrows)


def _moe(x2e, order_pad, wg, wu, wd, lng, lnb, tile_elo, tile_ehi, tile_nrows, tile_start):
    n_tok, de = x2e.shape
    n_tiles = tile_start.shape[0]
    tm, D = MOE_TILE, D_MODEL
    wlo = lambda s: pl.BlockSpec((1,) + s, lambda i, elo, ehi, nr, st, od: (elo[i], 0, 0))
    whi = lambda s: pl.BlockSpec((1,) + s, lambda i, elo, ehi, nr, st, od: (ehi[i], 0, 0))
    cst = lambda s: pl.BlockSpec(s, lambda i, elo, ehi, nr, st, od: (0, 0))
    gu, dn = (D, D_EXPERT), (D_EXPERT, D)
    grid_spec = pltpu.PrefetchScalarGridSpec(
        num_scalar_prefetch=5,
        grid=(n_tiles,),
        in_specs=[pl.BlockSpec(memory_space=pl.ANY),
                  wlo(gu), wlo(gu), wlo(dn), whi(gu), whi(gu), whi(dn),
                  cst(lng.shape), cst(lnb.shape)],
        out_specs=pl.BlockSpec(memory_space=pl.ANY),
        scratch_shapes=[
            pltpu.VMEM((2, tm // SUBLANES, SUBLANES, de), F32),
            pltpu.VMEM((2, tm // SUBLANES, SUBLANES, D), F32),
            pltpu.SemaphoreType.DMA((2,)),
            pltpu.SemaphoreType.DMA((2,)),
        ],
    )
    return pl.pallas_call(
        _moe_kernel,
        grid_spec=grid_spec,
        out_shape=jax.ShapeDtypeStruct((n_tok, D), F32),
        compiler_params=pltpu.CompilerParams(
            dimension_semantics=("arbitrary",), vmem_limit_bytes=VMEM_LIMIT),
        name="moe",
    )(tile_elo, tile_ehi, tile_nrows, tile_start, order_pad, x2e, wg, wu, wd, wg, wu, wd, lng, lnb)


def _dispatch_plan(bucket, n_tok):
    tm = MOE_TILE
    n_tiles = n_tok // tm + N_BUCKETS
    order = jnp.argsort(bucket, stable=True).astype(jnp.int32)
    kb = jnp.arange(N_BUCKETS, dtype=jnp.int32)
    counts = jnp.sum((bucket[:, None] == kb[None, :]).astype(jnp.int32), axis=0)
    tiles_per = (counts + tm - 1) // tm
    tile_end = jnp.cumsum(tiles_per)
    tile_start = tile_end - tiles_per
    sorted_start = jnp.cumsum(counts) - counts
    n_valid = tile_end[-1]
    ti = jnp.arange(n_tiles, dtype=jnp.int32)
    valid = ti < n_valid
    tile_bucket = jnp.sum((ti[:, None] >= tile_end[None, :]).astype(jnp.int32), axis=1)
    last_bucket = jnp.sum((n_valid - 1 >= tile_end).astype(jnp.int32))
    tile_bucket = jnp.minimum(jnp.where(valid, tile_bucket, last_bucket), N_BUCKETS - 1)
    pa = jnp.asarray([p[0] for p in PAIRS], jnp.int32)
    pb = jnp.asarray([p[1] for p in PAIRS], jnp.int32)
    grp, pair = tile_bucket // len(PAIRS), tile_bucket % len(PAIRS)
    tile_elo = grp * EXPERTS_PER_GROUP + pa[pair]
    tile_ehi = grp * EXPERTS_PER_GROUP + pb[pair]
    first_row = (ti - tile_start[tile_bucket]) * tm
    tile_nrows = jnp.where(valid, jnp.clip(counts[tile_bucket] - first_row, 0, tm), 0)
    start = jnp.clip(sorted_start[tile_bucket] + first_row, 0, n_tok)
    order_pad = jnp.concatenate([order, jnp.zeros((tm,), jnp.int32)])
    return (order_pad, tile_elo.astype(jnp.int32), tile_ehi.astype(jnp.int32),
            tile_nrows.astype(jnp.int32), start.astype(jnp.int32))


def kernel(x, mem, w_in, w_a2, b_a, gla_norm_g, w_s, b_s, sgu_ln_g, sgu_ln_b, w_out, wq_x, wk_x, wv_x,
           wo_x, w_router, b_router, w_gate, w_up, w_down, ln_g, ln_b):
    B, S, D = x.shape
    n_tok = B * S

    segs = (slice(0, 256), slice(256, 512), slice(512, 1024), slice(1024, 1536),
            slice(1552, 2064), slice(2064, 2576))
    w_in_r = jnp.concatenate(
        [w_in[:, :, s] for s in segs]
        + [jnp.pad(w_in[:, :, 1536:1552], ((0, 0), (0, 0), (0, A_PAD - GATE_RANK)))], axis=-1).astype(BF16)
    w_a2_p = jnp.pad(w_a2, ((0, 0), (0, A_PAD - GATE_RANK), (0, 0)))
    bs_full = jnp.broadcast_to(b_s[:, :, :, None], b_s.shape + (SGU_CH,))
    w_out_b, wq_b, wk_b, wv_b, wo_b = (w.astype(BF16) for w in (w_out, wq_x, wk_x, wv_x, wo_x))
    wg_b, wu_b, wd_b = (w.astype(BF16) for w in (w_gate, w_up, w_down))
    wr_t = w_router.T.astype(BF16)
    br_col = jnp.broadcast_to(b_router[:, None], (N_EXPERTS, TOK_TILE))
    row = lambda v: v.reshape(1, -1)

    for l in range(DEPTH):
        x = _mixer(x, w_in_r[l], w_a2_p[l], row(b_a[l]), row(gla_norm_g[l]), w_s[l], bs_full[l],
                   row(sgu_ln_g[l]), row(sgu_ln_b[l]), w_out_b[l], row(ln_g[l, 0]), row(ln_b[l, 0]))
        x2e, route = _xattn(x, mem, wq_b[l], wk_b[l], wv_b[l], wo_b[l], row(ln_g[l, 1]), row(ln_b[l, 1]),
                            wr_t, br_col)
        bucket = route[:, 0, :].reshape(-1).astype(jnp.int32)
        order_pad, t_elo, t_ehi, t_nrows, t_start = _dispatch_plan(bucket, n_tok)
        x = _moe(x2e.reshape(n_tok, D_EXT), order_pad, wg_b[l], wu_b[l], wd_b[l],
                 row(ln_g[l, 2]), row(ln_b[l, 2]), t_elo, t_ehi, t_nrows, t_start).reshape(B, S, D)
    return x
```

```python
import functools

import jax
import jax.numpy as jnp
from jax import lax
from jax.experimental import pallas as pl
from jax.experimental.pallas import tpu as pltpu

F32 = jnp.float32
BF16 = jnp.bfloat16

D_MODEL = 1024
DEPTH = 2
CHUNK = 64
N_MEM = 256
D_GLA = 512
D_SGU = 512
GLA_HEADS = 4
D_QK = 256
GLA_DK = 64
GLA_DV = 128
GATE_RANK = 16
GATE_TEMP = 16.0
SGU_GROUPS = 4
SGU_BLOCK = 128
SGU_CH = 128
XATTN_HEADS = 4
XATTN_DH = 256
N_EXPERTS = 16
N_EXPERT_GROUPS = 4
EXPERTS_PER_GROUP = 4
D_EXPERT = 512
DN_ALPHA = (2.0 * DEPTH) ** 0.25
LN_EPS = 1e-5
RMS_EPS = 1e-6

LANES = 128
SUBLANES = 8
A_PAD = LANES
C_Q, C_K, C_V, C_R = 0, 256, 512, 1024
C_U, C_VS, C_A = 1536, 2048, 2560
D_INP = C_A + A_PAD
D_EXT = D_MODEL + LANES
COL_GLO, COL_GHI = D_MODEL + 1, D_MODEL + 2

PAIRS = ((0, 1), (0, 2), (0, 3), (1, 2), (1, 3), (2, 3))
N_BUCKETS = N_EXPERT_GROUPS * len(PAIRS)

TOK_TILE = 512
MOE_TILE = 512
VMEM_LIMIT = 56 * 1024 * 1024

_HI = lax.Precision.HIGHEST


def _layer_norm(x, g, b):
    mu = jnp.mean(x, axis=-1, keepdims=True)
    xc = x - mu
    var = jnp.mean(xc * xc, axis=-1, keepdims=True)
    return xc * lax.rsqrt(var + LN_EPS) * g + b


def _gelu_tanh(x):
    return 0.5 * x * (1.0 + jnp.tanh(0.7978845608028654 * (x + 0.044715 * (x * x * x))))


def _silu(x):
    return x / (1.0 + jnp.exp(-x))


def _mixer_kernel(x_ref, xn_ref, win_ref, wa2_ref, ba_ref, gng_ref, ws_ref, bs_ref, slg_ref, slb_ref,
                  wout_ref, lng_ref, lnb_ref, o_ref, st_ref, dec_ref, upd_ref, snap_ref, p0_ref, p1_ref, g0_ref, g1_ref, y0_ref, y1_ref,
                  *, steps_per_seq):
    T = xn_ref.shape[0]
    g = pl.program_id(0)
    consts = (wa2_ref, ba_ref, gng_ref, ws_ref, bs_ref, slg_ref, slb_ref, wout_ref, lng_ref, lnb_ref)
    gla_scratch = (st_ref, dec_ref, upd_ref, snap_ref)

    def in_proj(x):
        return jnp.dot(x.astype(BF16), win_ref[...], preferred_element_type=F32)

    @pl.when(lax.rem(g, steps_per_seq) == 0)
    def _():
        st_ref[...] = jnp.zeros_like(st_ref)

    @pl.when(g == 0)
    def _():
        p0_ref[...] = in_proj(x_ref[pl.ds(0, T), :])

    lo, hi = pl.ds(0, T), pl.ds(T, T)
    p1_ref[...] = in_proj(x_ref[hi, :])
    o_ref[lo, :] = _mixer_gate_stage(x_ref.at[lo, :], p0_ref, *gla_scratch, g0_ref, y0_ref, *consts)
    p0_ref[...] = in_proj(xn_ref[...])
    o_ref[hi, :] = _mixer_gate_stage(x_ref.at[hi, :], p1_ref, *gla_scratch, g1_ref, y1_ref, *consts)


def _mixer_gate_stage(x_ref, p_ref, st_ref, dec_ref, upd_ref, snap_ref, g_ref, y_ref, wa2_ref, ba_ref, gng_ref, ws_ref, bs_ref,
                      slg_ref, slb_ref, wout_ref, lng_ref, lnb_ref):
    T = p_ref.shape[0]

    z = jnp.dot(p_ref[:, C_A:C_A + A_PAD], wa2_ref[...], precision=_HI,
                preferred_element_type=F32) + ba_ref[...]
    g_ref[...] = (jnp.minimum(z, 0.0) - jnp.log1p(jnp.exp(-jnp.abs(z)))) * (1.0 / GATE_TEMP)

    ri = lax.broadcasted_iota(jnp.int32, (CHUNK, CHUNK), 0)
    ci = lax.broadcasted_iota(jnp.int32, (CHUNK, CHUNK), 1)
    tri = (ci <= ri).astype(F32)
    hr = lax.broadcasted_iota(jnp.int32, (D_GLA, D_QK), 0) // GLA_DV
    hc = lax.broadcasted_iota(jnp.int32, (D_GLA, D_QK), 1) // GLA_DK
    head_mask = (hr == hc).astype(F32)

    n_chunks = T // CHUNK
    for c in range(n_chunks):
        rows = pl.ds(c * CHUNK, CHUNK)
        bcum = jnp.dot(tri, g_ref[rows, :], precision=_HI, preferred_element_type=F32)
        b_last = bcum[CHUNK - 1:CHUNK, :]
        dec_ref[c] = jnp.exp(b_last)
        k_dec = p_ref[rows, C_K:C_K + D_QK] * jnp.exp(b_last - bcum)
        v_c = p_ref[rows, C_V:C_V + D_GLA]
        upd = lax.dot_general(v_c.astype(BF16), k_dec.astype(BF16), (((0,), (0,)), ((), ())),
                              preferred_element_type=F32)
        upd_ref[c] = upd * head_mask
    for c in range(n_chunks):
        st = st_ref[...] * dec_ref[c] + upd_ref[c]
        st_ref[...] = st
        snap_ref[c] = st.astype(BF16)
    for c in range(n_chunks):
        rows = pl.ds(c * CHUNK, CHUNK)
        q_c = p_ref[rows, C_Q:C_Q + D_QK] * (GLA_DK ** -0.5)
        o_c = lax.dot_general(q_c.astype(BF16), snap_ref[c], (((1,), (1,)), ((), ())),
                              preferred_element_type=F32)
        r_c = p_ref[rows, C_R:C_R + D_GLA]
        for h in range(GLA_HEADS):
            cols = slice(h * GLA_DV, (h + 1) * GLA_DV)
            oh = o_c[:, cols]
            oh = oh * lax.rsqrt(jnp.mean(oh * oh, axis=-1, keepdims=True) + RMS_EPS)
            y_ref[rows, cols] = (oh * gng_ref[:, cols] * _silu(r_c[:, cols])).astype(BF16)

    ti = lax.broadcasted_iota(jnp.int32, (SGU_BLOCK, SGU_BLOCK), 0) // CHUNK
    si = lax.broadcasted_iota(jnp.int32, (SGU_BLOCK, SGU_BLOCK), 1) // CHUNK
    causal = si <= ti
    for gi in range(SGU_GROUPS):
        w_m = jnp.where(causal, ws_ref[gi], 0.0).astype(BF16)
        cu = slice(C_U + gi * SGU_CH, C_U + (gi + 1) * SGU_CH)
        cv = slice(C_VS + gi * SGU_CH, C_VS + (gi + 1) * SGU_CH)
        cg = slice(gi * SGU_CH, (gi + 1) * SGU_CH)
        for n in range(T // SGU_BLOCK):
            rows = pl.ds(n * SGU_BLOCK, SGU_BLOCK)
            u = _gelu_tanh(p_ref[rows, cu])
            vn = _layer_norm(_gelu_tanh(p_ref[rows, cv]), slg_ref[:, cg], slb_ref[:, cg])
            mixed = jnp.dot(w_m, vn.astype(BF16), preferred_element_type=F32) + bs_ref[gi]
            y_ref[rows, D_GLA + gi * SGU_CH:D_GLA + (gi + 1) * SGU_CH] = (u * mixed).astype(BF16)

    h = jnp.dot(y_ref[...], wout_ref[...], preferred_element_type=F32)
    return _layer_norm(DN_ALPHA * x_ref[...] + h, lng_ref[...], lnb_ref[...])


def _const_spec(shape, single_buffer=False):
    nd = len(shape)
    if single_buffer:
        return pl.BlockSpec(shape, lambda *_: (0,) * nd, pipeline_mode=pl.Buffered(1))
    return pl.BlockSpec(shape, lambda *_: (0,) * nd)


def _mixer(x, win, wa2, ba, gng, ws, bs_full, slg, slb, wout, lng, lnb):
    B, S, D = x.shape
    T = TOK_TILE
    n_tiles = B * S // T
    assert S % (2 * T) == 0
    x2d = x.reshape(B * S, D)
    consts = (win, wa2, ba, gng, ws, bs_full, slg, slb, wout, lng, lnb)
    pair = pl.BlockSpec((2 * T, D), lambda g: (g, 0))
    nxt = pl.BlockSpec((T, D), lambda g: (jnp.minimum(2 * g + 2, n_tiles - 1), 0))
    out = pl.pallas_call(
        functools.partial(_mixer_kernel, steps_per_seq=S // (2 * T)),
        grid=(n_tiles // 2,),
        in_specs=[pair, nxt] + [_const_spec(c.shape, single_buffer=True) for c in consts],
        out_specs=pair,
        out_shape=jax.ShapeDtypeStruct((B * S, D), F32),
        scratch_shapes=[
            pltpu.VMEM((D_GLA, D_QK), F32),
            pltpu.VMEM((T // CHUNK, 1, D_QK), F32),
            pltpu.VMEM((T // CHUNK, D_GLA, D_QK), F32),
            pltpu.VMEM((T // CHUNK, D_GLA, D_QK), BF16),
            pltpu.VMEM((T, D_INP), F32),
            pltpu.VMEM((T, D_INP), F32),
            pltpu.VMEM((T, D_QK), F32),
            pltpu.VMEM((T, D_QK), F32),
            pltpu.VMEM((T, D), BF16),
            pltpu.VMEM((T, D), BF16),
        ],
        compiler_params=pltpu.CompilerParams(
            dimension_semantics=("arbitrary",), vmem_limit_bytes=VMEM_LIMIT),
        name="mixer",
    )(x2d, x2d, *consts)
    return out.reshape(B, S, D)


def _route(logits):
    m = functools.reduce(jnp.maximum, logits)
    e = [jnp.exp(l - m) for l in logits]
    inv = 1.0 / functools.reduce(jnp.add, e)
    sc = [v * inv for v in e]

    def top2(a):
        first = functools.reduce(jnp.maximum, a)
        second = None
        for i in range(len(a)):
            for j in range(i + 1, len(a)):
                mn = jnp.minimum(a[i], a[j])
                second = mn if second is None else jnp.maximum(second, mn)
        return first + second

    gs = [top2(sc[g * EXPERTS_PER_GROUP:(g + 1) * EXPERTS_PER_GROUP]) for g in range(N_EXPERT_GROUPS)]
    best, g_sel = gs[0], jnp.zeros_like(gs[0])
    for g in range(1, N_EXPERT_GROUPS):
        better = gs[g] > best
        g_sel = jnp.where(better, float(g), g_sel)
        best = jnp.where(better, gs[g], best)
    a = []
    for j in range(EXPERTS_PER_GROUP):
        v = jnp.zeros_like(best)
        for g in range(N_EXPERT_GROUPS):
            v = v + jnp.where(g_sel == float(g), sc[g * EXPERTS_PER_GROUP + j], 0.0)
        a.append(v)
    w1, i1 = a[0], jnp.zeros_like(a[0])
    for j in range(1, EXPERTS_PER_GROUP):
        better = a[j] > w1
        i1 = jnp.where(better, float(j), i1)
        w1 = jnp.where(better, a[j], w1)
    w2, i2 = None, None
    for j in range(EXPERTS_PER_GROUP):
        cand = jnp.where(i1 == float(j), -1.0, a[j])
        if w2 is None:
            w2, i2 = cand, jnp.zeros_like(cand)
        else:
            better = cand > w2
            i2 = jnp.where(better, float(j), i2)
            w2 = jnp.where(better, cand, w2)
    tot = w1 + w2
    w1n, w2n = w1 / tot, w2 / tot
    first_is_lo = i1 < i2
    lo = jnp.where(first_is_lo, i1, i2)
    hi = jnp.where(first_is_lo, i2, i1)
    pair = jnp.zeros_like(lo)
    for pi, (pa, pb) in enumerate(PAIRS):
        pair = jnp.where((lo == float(pa)) & (hi == float(pb)), float(pi), pair)
    bucket = g_sel * float(len(PAIRS)) + pair
    g_lo = jnp.where(first_is_lo, w1n, w2n)
    g_hi = jnp.where(first_is_lo, w2n, w1n)
    return bucket, g_lo, g_hi


def _xattn_kernel(x_ref, mem_ref, wq_ref, wk_ref, wv_ref, wo_ref, lng_ref, lnb_ref, wr_ref, br_ref,
                  o_ref, route_ref, k_ref, v_ref, a_ref):
    @pl.when(pl.program_id(1) == 0)
    def _():
        mb = mem_ref[0].astype(BF16)
        k_ref[...] = jnp.dot(mb, wk_ref[...], preferred_element_type=F32).astype(BF16)
        v_ref[...] = jnp.dot(mb, wv_ref[...], preferred_element_type=F32).astype(BF16)

    x = x_ref[0]
    q = jnp.dot(x.astype(BF16), wq_ref[...], preferred_element_type=F32).astype(BF16)
    for h in range(XATTN_HEADS):
        cols = slice(h * XATTN_DH, (h + 1) * XATTN_DH)
        s = lax.dot_general(q[:, cols], k_ref[:, cols], (((1,), (1,)), ((), ())),
                            preferred_element_type=F32) * (XATTN_DH ** -0.5)
        e = jnp.exp(s - jnp.max(s, axis=-1, keepdims=True))
        p = e / jnp.sum(e, axis=-1, keepdims=True)
        a_ref[:, cols] = jnp.dot(p.astype(BF16), v_ref[:, cols],
                                 preferred_element_type=F32).astype(BF16)
    hres = jnp.dot(a_ref[...], wo_ref[...], preferred_element_type=F32)
    x2 = _layer_norm(DN_ALPHA * x + hres, lng_ref[...], lnb_ref[...])
    o_ref[0, :, 0:D_MODEL] = x2

    lt = lax.dot_general(wr_ref[...], x2.astype(BF16), (((1,), (1,)), ((), ())),
                         preferred_element_type=F32) + br_ref[...]
    bucket, g_lo, g_hi = _route([lt[i:i + 1, :] for i in range(N_EXPERTS)])
    route_ref[0, 0:1, :] = bucket
    route_ref[0, 1:2, :] = g_lo
    route_ref[0, 2:3, :] = g_hi
    route_ref[0, 3:8, :] = jnp.zeros((5, bucket.shape[1]), F32)
    rt = jnp.concatenate([bucket, g_lo, g_hi, jnp.zeros((LANES - 3, bucket.shape[1]), F32)], axis=0)
    o_ref[0, :, D_MODEL:D_EXT] = rt.T


def _xattn(x, mem, wq, wk, wv, wo, lng, lnb, wr_t, br_col):
    B, S, D = x.shape
    T = TOK_TILE
    nt = S // T
    tile = pl.BlockSpec((1, T, D), lambda b, s: (b, s, 0))
    consts = (wq, wk, wv, wo, lng, lnb, wr_t, br_col)
    return pl.pallas_call(
        _xattn_kernel,
        grid=(B, nt),
        in_specs=[tile, pl.BlockSpec((1, N_MEM, D), lambda b, s: (b, 0, 0))]
        + [_const_spec(c.shape) for c in consts],
        out_specs=[pl.BlockSpec((1, T, D_EXT), lambda b, s: (b, s, 0)),
                   pl.BlockSpec((1, 8, T), lambda b, s: (b * nt + s, 0, 0))],
        out_shape=[jax.ShapeDtypeStruct((B, S, D_EXT), F32),
                   jax.ShapeDtypeStruct((B * nt, 8, T), F32)],
        scratch_shapes=[
            pltpu.VMEM((N_MEM, D), BF16),
            pltpu.VMEM((N_MEM, D), BF16),
            pltpu.VMEM((T, D), BF16),
        ],
        compiler_params=pltpu.CompilerParams(
            dimension_semantics=("arbitrary", "arbitrary"), vmem_limit_bytes=VMEM_LIMIT),
        name="xattn",
    )(x, mem, *consts)


def _moe_kernel(elo_ref, ehi_ref, nrows_ref, start_ref, order_ref, x_hbm,
                wg_lo, wu_lo, wd_lo, wg_hi, wu_hi, wd_hi, lng_ref, lnb_ref, o_hbm,
                xbuf, obuf, gsem, ssem):
    del elo_ref, ehi_ref
    i = pl.program_id(0)
    n_tiles = pl.num_programs(0)
    tm = xbuf.shape[1] * SUBLANES
    slot = lax.rem(i, 2)
    nrows = nrows_ref[i]
    nxt = jnp.minimum(i + 1, n_tiles - 1)
    has_next = jnp.logical_and(i + 1 < n_tiles, nrows_ref[nxt] > 0)

    def start_gather(tile, s):
        base = start_ref[tile]

        def body(j, c):
            for k in range(SUBLANES):
                t = order_ref[base + j * SUBLANES + k]
                pltpu.make_async_copy(x_hbm.at[pl.ds(t, 1)], xbuf.at[s, j, pl.ds(k, 1)],
                                      gsem.at[s]).start(k % 2)
            return c
        lax.fori_loop(0, tm // SUBLANES, body, 0)

    def start_scatter(s, n):
        base = start_ref[i]

        @pl.when(n == tm)
        def _():
            def body(j, c):
                for k in range(SUBLANES):
                    t = order_ref[base + j * SUBLANES + k]
                    pltpu.make_async_copy(obuf.at[s, j, pl.ds(k, 1)], o_hbm.at[pl.ds(t, 1)],
                                          ssem.at[s]).start(k % 2)
                return c
            lax.fori_loop(0, tm // SUBLANES, body, 0)

        @pl.when(n < tm)
        def _():
            def body(r, c):
                t = order_ref[base + r]
                pltpu.make_async_copy(obuf.at[s, r // SUBLANES, pl.ds(r % SUBLANES, 1)],
                                      o_hbm.at[pl.ds(t, 1)], ssem.at[s]).start()
                return c
            lax.fori_loop(0, n, body, 0)

    def wait_scatter(s, n):
        @pl.when(n == tm)
        def _():
            pltpu.make_async_copy(obuf.at[s], obuf.at[s], ssem.at[s]).wait()

        @pl.when(n < tm)
        def _():
            def body(r, c):
                pltpu.make_async_copy(obuf.at[s, 0, pl.ds(0, 1)], obuf.at[s, 0, pl.ds(0, 1)],
                                      ssem.at[s]).wait()
                return c
            lax.fori_loop(0, n, body, 0)

    @pl.when(jnp.logical_and(i == 0, nrows > 0))
    def _():
        start_gather(0, 0)

    @pl.when(has_next)
    def _():
        start_gather(nxt, 1 - slot)

    @pl.when(nrows > 0)
    def _():
        pltpu.make_async_copy(xbuf.at[slot], xbuf.at[slot], gsem.at[slot]).wait()
        xe = xbuf[slot].reshape(tm, xbuf.shape[3])
        x = xe[:, 0:D_MODEL]
        g_lo = xe[:, COL_GLO:COL_GLO + 1]
        g_hi = xe[:, COL_GHI:COL_GHI + 1]
        xb = x.astype(BF16)

        def expert(wg, wu, wd):
            hg = jnp.dot(xb, wg[0], preferred_element_type=F32)
            hu = jnp.dot(xb, wu[0], preferred_element_type=F32)
            return jnp.dot((_silu(hg) * hu).astype(BF16), wd[0], preferred_element_type=F32)

        y = g_lo * expert(wg_lo, wu_lo, wd_lo)
        y = y + g_hi * expert(wg_hi, wu_hi, wd_hi)
        res = _layer_norm(DN_ALPHA * x + y, lng_ref[...], lnb_ref[...])
        obuf[slot] = res.reshape(tm // SUBLANES, SUBLANES, D_MODEL)
        start_scatter(slot, nrows)

        @pl.when(i > 0)
        def _():
            wait_scatter(1 - slot, nrows_ref[jnp.maximum(i - 1, 0)])

        @pl.when(jnp.logical_not(has_next))
        def _():
            wait_scatter(slot, nrows)


def _moe(x2e, order_pad, wg, wu, wd, lng, lnb, tile_elo, tile_ehi, tile_nrows, tile_start):
    n_tok, de = x2e.shape
    n_tiles = tile_start.shape[0]
    tm, D = MOE_TILE, D_MODEL
    wlo = lambda s: pl.BlockSpec((1,) + s, lambda i, elo, ehi, nr, st, od: (elo[i], 0, 0))
    whi = lambda s: pl.BlockSpec((1,) + s, lambda i, elo, ehi, nr, st, od: (ehi[i], 0, 0))
    cst = lambda s: pl.BlockSpec(s, lambda i, elo, ehi, nr, st, od: (0, 0))
    gu, dn = (D, D_EXPERT), (D_EXPERT, D)
    grid_spec = pltpu.PrefetchScalarGridSpec(
        num_scalar_prefetch=5,
        grid=(n_tiles,),
        in_specs=[pl.BlockSpec(memory_space=pl.ANY),
                  wlo(gu), wlo(gu), wlo(dn), whi(gu), whi(gu), whi(dn),
                  cst(lng.shape), cst(lnb.shape)],
        out_specs=pl.BlockSpec(memory_space=pl.ANY),
        scratch_shapes=[
            pltpu.VMEM((2, tm // SUBLANES, SUBLANES, de), F32),
            pltpu.VMEM((2, tm // SUBLANES, SUBLANES, D), F32),
            pltpu.SemaphoreType.DMA((2,)),
            pltpu.SemaphoreType.DMA((2,)),
        ],
    )
    return pl.pallas_call(
        _moe_kernel,
        grid_spec=grid_spec,
        out_shape=jax.ShapeDtypeStruct((n_tok, D), F32),
        compiler_params=pltpu.CompilerParams(
            dimension_semantics=("arbitrary",), vmem_limit_bytes=VMEM_LIMIT),
        name="moe",
    )(tile_elo, tile_ehi, tile_nrows, tile_start, order_pad, x2e, wg, wu, wd, wg, wu, wd, lng, lnb)


def _dispatch_plan(bucket, n_tok):
    tm = MOE_TILE
    n_tiles = n_tok // tm + N_BUCKETS
    order = jnp.argsort(bucket, stable=True).astype(jnp.int32)
    kb = jnp.arange(N_BUCKETS, dtype=jnp.int32)
    counts = jnp.sum((bucket[:, None] == kb[None, :]).astype(jnp.int32), axis=0)
    tiles_per = (counts + tm - 1) // tm
    tile_end = jnp.cumsum(tiles_per)
    tile_start = tile_end - tiles_per
    sorted_start = jnp.cumsum(counts) - counts
    n_valid = tile_end[-1]
    ti = jnp.arange(n_tiles, dtype=jnp.int32)
    valid = ti < n_valid
    tile_bucket = jnp.sum((ti[:, None] >= tile_end[None, :]).astype(jnp.int32), axis=1)
    last_bucket = jnp.sum((n_valid - 1 >= tile_end).astype(jnp.int32))
    tile_bucket = jnp.minimum(jnp.where(valid, tile_bucket, last_bucket), N_BUCKETS - 1)
    pa = jnp.asarray([p[0] for p in PAIRS], jnp.int32)
    pb = jnp.asarray([p[1] for p in PAIRS], jnp.int32)
    grp, pair = tile_bucket // len(PAIRS), tile_bucket % len(PAIRS)
    tile_elo = grp * EXPERTS_PER_GROUP + pa[pair]
    tile_ehi = grp * EXPERTS_PER_GROUP + pb[pair]
    first_row = (ti - tile_start[tile_bucket]) * tm
    tile_nrows = jnp.where(valid, jnp.clip(counts[tile_bucket] - first_row, 0, tm), 0)
    start = jnp.clip(sorted_start[tile_bucket] + first_row, 0, n_tok)
    order_pad = jnp.concatenate([order, jnp.zeros((tm,), jnp.int32)])
    return (order_pad, tile_elo.astype(jnp.int32), tile_ehi.astype(jnp.int32),
            tile_nrows.astype(jnp.int32), start.astype(jnp.int32))


def kernel(x, mem, w_in, w_a2, b_a, gla_norm_g, w_s, b_s, sgu_ln_g, sgu_ln_b, w_out, wq_x, wk_x, wv_x,
           wo_x, w_router, b_router, w_gate, w_up, w_down, ln_g, ln_b):
    B, S, D = x.shape
    n_tok = B * S

    segs = (slice(0, 256), slice(256, 512), slice(512, 1024), slice(1024, 1536),
            slice(1552, 2064), slice(2064, 2576))
    w_in_r = jnp.concatenate(
        [w_in[:, :, s] for s in segs]
        + [jnp.pad(w_in[:, :, 1536:1552], ((0, 0), (0, 0), (0, A_PAD - GATE_RANK)))], axis=-1).astype(BF16)
    w_a2_p = jnp.pad(w_a2, ((0, 0), (0, A_PAD - GATE_RANK), (0, 0)))
    bs_full = jnp.broadcast_to(b_s[:, :, :, None], b_s.shape + (SGU_CH,))
    w_out_b, wq_b, wk_b, wv_b, wo_b = (w.astype(BF16) for w in (w_out, wq_x, wk_x, wv_x, wo_x))
    wg_b, wu_b, wd_b = (w.astype(BF16) for w in (w_gate, w_up, w_down))
    wr_t = w_router.T.astype(BF16)
    br_col = jnp.broadcast_to(b_router[:, None], (N_EXPERTS, TOK_TILE))
    row = lambda v: v.reshape(1, -1)

    for l in range(DEPTH):
        x = _mixer(x, w_in_r[l], w_a2_p[l], row(b_a[l]), row(gla_norm_g[l]), w_s[l], bs_full[l],
                   row(sgu_ln_g[l]), row(sgu_ln_b[l]), w_out_b[l], row(ln_g[l, 0]), row(ln_b[l, 0]))
        x2e, route = _xattn(x, mem, wq_b[l], wk_b[l], wv_b[l], wo_b[l], row(ln_g[l, 1]), row(ln_b[l, 1]),
                            wr_t, br_col)
        bucket = route[:, 0, :].reshape(-1).astype(jnp.int32)
        order_pad, t_elo, t_ehi, t_nrows, t_start = _dispatch_plan(bucket, n_tok)
        x = _moe(x2e.reshape(n_tok, D_EXT), order_pad, wg_b[l], wu_b[l], wd_b[l],
                 row(ln_g[l, 2]), row(ln_b[l, 2]), t_elo, t_ehi, t_nrows, t_start).reshape(B, S, D)
    return x
```

```python
import functools

import jax
import jax.numpy as jnp
from jax import lax
from jax.experimental import pallas as pl
from jax.experimental.pallas import tpu as pltpu

F32 = jnp.float32
BF16 = jnp.bfloat16

D_MODEL = 1024
DEPTH = 2
CHUNK = 64
N_MEM = 256
D_GLA = 512
D_SGU = 512
GLA_HEADS = 4
D_QK = 256
GLA_DK = 64
GLA_DV = 128
GATE_RANK = 16
GATE_TEMP = 16.0
SGU_GROUPS = 4
SGU_BLOCK = 128
SGU_CH = 128
XATTN_HEADS = 4
XATTN_DH = 256
N_EXPERTS = 16
N_EXPERT_GROUPS = 4
EXPERTS_PER_GROUP = 4
D_EXPERT = 512
DN_ALPHA = (2.0 * DEPTH) ** 0.25
LN_EPS = 1e-5
RMS_EPS = 1e-6

LANES = 128
SUBLANES = 8
A_PAD = LANES
C_Q, C_K, C_V, C_R = 0, 256, 512, 1024
C_U, C_VS, C_A = 1536, 2048, 2560
D_INP = C_A + A_PAD
D_EXT = D_MODEL + LANES
COL_GLO, COL_GHI = D_MODEL + 1, D_MODEL + 2

PAIRS = ((0, 1), (0, 2), (0, 3), (1, 2), (1, 3), (2, 3))
N_BUCKETS = N_EXPERT_GROUPS * len(PAIRS)

TOK_TILE = 512
MOE_TILE = 512
VMEM_LIMIT = 56 * 1024 * 1024


def _layer_norm(x, g, b):
    mu = jnp.mean(x, axis=-1, keepdims=True)
    xc = x - mu
    var = jnp.mean(xc * xc, axis=-1, keepdims=True)
    return xc * lax.rsqrt(var + LN_EPS) * g + b


def _gelu_tanh(x):
    return 0.5 * x * (1.0 + jnp.tanh(0.7978845608028654 * (x + 0.044715 * (x * x * x))))


def _silu(x):
    return x / (1.0 + jnp.exp(-x))


def _mixer_kernel(x_ref, xn_ref, win_ref, wa2_ref, ba_ref, gng_ref, ws_ref, bs_ref, slg_ref, slb_ref,
                  wout_ref, lng_ref, lnb_ref, o_ref, st_ref, dec_ref, upd_ref, snap_ref, p0_ref, p1_ref,
                  g0_ref, g1_ref, y0_ref, y1_ref, *, steps_per_seq):
    T = xn_ref.shape[0]
    g = pl.program_id(0)
    consts = (wa2_ref, ba_ref, gng_ref, ws_ref, bs_ref, slg_ref, slb_ref, wout_ref, lng_ref, lnb_ref)
    gla_scratch = (st_ref, dec_ref, upd_ref, snap_ref)

    @pl.when(lax.rem(g, steps_per_seq) == 0)
    def _():
        st_ref[...] = jnp.zeros_like(st_ref)

    @pl.when(g == 0)
    def _():
        p0_ref[...] = jnp.dot(x_ref[pl.ds(0, T), :].astype(BF16), win_ref[...], preferred_element_type=F32)

    def in_proj(x):
        return jnp.dot(x.astype(BF16), win_ref[...], preferred_element_type=F32)

    lo, hi = pl.ds(0, T), pl.ds(T, T)
    p1_ref[...] = in_proj(x_ref[hi, :])
    o_ref[lo, :] = _mixer_gate_stage(x_ref.at[lo, :], p0_ref, *gla_scratch, g0_ref, y0_ref, *consts)
    p0_ref[...] = in_proj(xn_ref[...])
    o_ref[hi, :] = _mixer_gate_stage(x_ref.at[hi, :], p1_ref, *gla_scratch, g1_ref, y1_ref, *consts)


def _mixer_gate_stage(x_ref, p_ref, st_ref, dec_ref, upd_ref, snap_ref, g_ref, y_ref, wa2_ref, ba_ref, gng_ref,
                      ws_ref, bs_ref, slg_ref, slb_ref, wout_ref, lng_ref, lnb_ref):
    T = p_ref.shape[0]

    z = jnp.dot(p_ref[:, C_A:C_A + A_PAD].astype(BF16), wa2_ref[...],
                preferred_element_type=F32) + ba_ref[...]
    g_ref[...] = (jnp.minimum(z, 0.0) - jnp.log1p(jnp.exp(-jnp.abs(z)))) * (1.0 / GATE_TEMP)

    ri = lax.broadcasted_iota(jnp.int32, (CHUNK, 3 * CHUNK), 0)
    ci = lax.broadcasted_iota(jnp.int32, (CHUNK, 3 * CHUNK), 1) % CHUNK
    tri3 = (ci <= ri).astype(BF16)

    def cumsum_chunk(gc):
        hi = gc.astype(BF16)
        r1 = gc - hi.astype(F32)
        mid = r1.astype(BF16)
        lo = (r1 - mid.astype(F32)).astype(BF16)
        return jnp.dot(tri3, jnp.concatenate([hi, mid, lo], axis=0), preferred_element_type=F32)
    hr = lax.broadcasted_iota(jnp.int32, (D_GLA, D_QK), 0) // GLA_DV
    hc = lax.broadcasted_iota(jnp.int32, (D_GLA, D_QK), 1) // GLA_DK
    head_mask = (hr == hc).astype(F32)

    n_chunks = T // CHUNK
    for c in range(n_chunks):
        rows = pl.ds(c * CHUNK, CHUNK)
        bcum = cumsum_chunk(g_ref[rows, :])
        b_last = bcum[CHUNK - 1:CHUNK, :]
        dec_ref[c] = jnp.exp(b_last)
        k_dec = p_ref[rows, C_K:C_K + D_QK] * jnp.exp(b_last - bcum)
        v_c = p_ref[rows, C_V:C_V + D_GLA]
        upd = lax.dot_general(v_c.astype(BF16), k_dec.astype(BF16), (((0,), (0,)), ((), ())),
                              preferred_element_type=F32)
        upd_ref[c] = upd * head_mask
    for c in range(n_chunks):
        st = st_ref[...] * dec_ref[c] + upd_ref[c]
        st_ref[...] = st
        snap_ref[c] = st.astype(BF16)
    for c in range(n_chunks):
        rows = pl.ds(c * CHUNK, CHUNK)
        q_c = p_ref[rows, C_Q:C_Q + D_QK] * (GLA_DK ** -0.5)
        o_c = lax.dot_general(q_c.astype(BF16), snap_ref[c], (((1,), (1,)), ((), ())),
                              preferred_element_type=F32)
        r_c = p_ref[rows, C_R:C_R + D_GLA]
        for h in range(GLA_HEADS):
            cols = slice(h * GLA_DV, (h + 1) * GLA_DV)
            oh = o_c[:, cols]
            oh = oh * lax.rsqrt(jnp.mean(oh * oh, axis=-1, keepdims=True) + RMS_EPS)
            y_ref[rows, cols] = (oh * gng_ref[:, cols] * _silu(r_c[:, cols])).astype(BF16)

    ti = lax.broadcasted_iota(jnp.int32, (SGU_BLOCK, SGU_BLOCK), 0) // CHUNK
    si = lax.broadcasted_iota(jnp.int32, (SGU_BLOCK, SGU_BLOCK), 1) // CHUNK
    causal = si <= ti
    for gi in range(SGU_GROUPS):
        w_m = jnp.where(causal, ws_ref[gi], 0.0).astype(BF16)
        cu = slice(C_U + gi * SGU_CH, C_U + (gi + 1) * SGU_CH)
        cv = slice(C_VS + gi * SGU_CH, C_VS + (gi + 1) * SGU_CH)
        cg = slice(gi * SGU_CH, (gi + 1) * SGU_CH)
        n_blocks = T // SGU_BLOCK
        vn = [_layer_norm(_gelu_tanh(p_ref[pl.ds(n * SGU_BLOCK, SGU_BLOCK), cv]),
                          slg_ref[:, cg], slb_ref[:, cg]).astype(BF16) for n in range(n_blocks)]
        mixed = jnp.dot(w_m, jnp.concatenate(vn, axis=1), preferred_element_type=F32)
        for n in range(n_blocks):
            rows = pl.ds(n * SGU_BLOCK, SGU_BLOCK)
            u = _gelu_tanh(p_ref[rows, cu])
            m_n = mixed[:, n * SGU_CH:(n + 1) * SGU_CH] + bs_ref[gi]
            y_ref[rows, D_GLA + gi * SGU_CH:D_GLA + (gi + 1) * SGU_CH] = (u * m_n).astype(BF16)

    h = jnp.dot(y_ref[...], wout_ref[...], preferred_element_type=F32)
    return _layer_norm(DN_ALPHA * x_ref[...] + h, lng_ref[...], lnb_ref[...])


def _const_spec(shape, single_buffer=False):
    nd = len(shape)
    if single_buffer:
        return pl.BlockSpec(shape, lambda *_: (0,) * nd, pipeline_mode=pl.Buffered(1))
    return pl.BlockSpec(shape, lambda *_: (0,) * nd)


def _mixer(x, win, wa2, ba, gng, ws, bs_full, slg, slb, wout, lng, lnb):
    B, S, D = x.shape
    T = TOK_TILE
    n_tiles = B * S // T
    assert S % (2 * T) == 0
    x2d = x.reshape(B * S, D)
    consts = (win, wa2, ba, gng, ws, bs_full, slg, slb, wout, lng, lnb)
    pair = pl.BlockSpec((2 * T, D), lambda g: (g, 0))
    nxt = pl.BlockSpec((T, D), lambda g: (jnp.minimum(2 * g + 2, n_tiles - 1), 0))
    out = pl.pallas_call(
        functools.partial(_mixer_kernel, steps_per_seq=S // (2 * T)),
        grid=(n_tiles // 2,),
        in_specs=[pair, nxt] + [_const_spec(c.shape, single_buffer=True) for c in consts],
        out_specs=pair,
        out_shape=jax.ShapeDtypeStruct((B * S, D), F32),
        scratch_shapes=[
            pltpu.VMEM((D_GLA, D_QK), F32),
            pltpu.VMEM((T // CHUNK, 1, D_QK), F32),
            pltpu.VMEM((T // CHUNK, D_GLA, D_QK), F32),
            pltpu.VMEM((T // CHUNK, D_GLA, D_QK), BF16),
            pltpu.VMEM((T, D_INP), F32),
            pltpu.VMEM((T, D_INP), F32),
            pltpu.VMEM((T, D_QK), F32),
            pltpu.VMEM((T, D_QK), F32),
            pltpu.VMEM((T, D), BF16),
            pltpu.VMEM((T, D), BF16),
        ],
        compiler_params=pltpu.CompilerParams(
            dimension_semantics=("arbitrary",), vmem_limit_bytes=VMEM_LIMIT),
        name="mixer",
    )(x2d, x2d, *consts)
    return out.reshape(B, S, D)


def _route(logits):
    m = functools.reduce(jnp.maximum, logits)
    e = [jnp.exp(l - m) for l in logits]
    inv = 1.0 / functools.reduce(jnp.add, e)
    sc = [v * inv for v in e]

    def top2(a):
        first = functools.reduce(jnp.maximum, a)
        second = None
        for i in range(len(a)):
            for j in range(i + 1, len(a)):
                mn = jnp.minimum(a[i], a[j])
                second = mn if second is None else jnp.maximum(second, mn)
        return first + second

    gs = [top2(sc[g * EXPERTS_PER_GROUP:(g + 1) * EXPERTS_PER_GROUP]) for g in range(N_EXPERT_GROUPS)]
    best, g_sel = gs[0], jnp.zeros_like(gs[0])
    for g in range(1, N_EXPERT_GROUPS):
        better = gs[g] > best
        g_sel = jnp.where(better, float(g), g_sel)
        best = jnp.where(better, gs[g], best)
    a = []
    for j in range(EXPERTS_PER_GROUP):
        v = jnp.zeros_like(best)
        for g in range(N_EXPERT_GROUPS):
            v = v + jnp.where(g_sel == float(g), sc[g * EXPERTS_PER_GROUP + j], 0.0)
        a.append(v)
    w1, i1 = a[0], jnp.zeros_like(a[0])
    for j in range(1, EXPERTS_PER_GROUP):
        better = a[j] > w1
        i1 = jnp.where(better, float(j), i1)
        w1 = jnp.where(better, a[j], w1)
    w2, i2 = None, None
    for j in range(EXPERTS_PER_GROUP):
        cand = jnp.where(i1 == float(j), -1.0, a[j])
        if w2 is None:
            w2, i2 = cand, jnp.zeros_like(cand)
        else:
            better = cand > w2
            i2 = jnp.where(better, float(j), i2)
            w2 = jnp.where(better, cand, w2)
    tot = w1 + w2
    w1n, w2n = w1 / tot, w2 / tot
    first_is_lo = i1 < i2
    lo = jnp.where(first_is_lo, i1, i2)
    hi = jnp.where(first_is_lo, i2, i1)
    pair = jnp.zeros_like(lo)
    for pi, (pa, pb) in enumerate(PAIRS):
        pair = jnp.where((lo == float(pa)) & (hi == float(pb)), float(pi), pair)
    bucket = g_sel * float(len(PAIRS)) + pair
    g_lo = jnp.where(first_is_lo, w1n, w2n)
    g_hi = jnp.where(first_is_lo, w2n, w1n)
    return bucket, g_lo, g_hi


def _xattn_kernel(x_ref, mem_ref, wq_ref, wk_ref, wv_ref, wo_ref, lng_ref, lnb_ref, wr_ref, br_ref,
                  o_ref, route_ref, k_ref, v_ref, a_ref):
    @pl.when(pl.program_id(1) == 0)
    def _():
        mb = mem_ref[0].astype(BF16)
        k_ref[...] = jnp.dot(mb, wk_ref[...], preferred_element_type=F32).astype(BF16)
        v_ref[...] = jnp.dot(mb, wv_ref[...], preferred_element_type=F32).astype(BF16)

    x = x_ref[0]
    q = jnp.dot(x.astype(BF16), wq_ref[...], preferred_element_type=F32).astype(BF16)
    for h in range(XATTN_HEADS):
        cols = slice(h * XATTN_DH, (h + 1) * XATTN_DH)
        s = lax.dot_general(q[:, cols], k_ref[:, cols], (((1,), (1,)), ((), ())),
                            preferred_element_type=F32) * (XATTN_DH ** -0.5)
        e = jnp.exp(s - jnp.max(s, axis=-1, keepdims=True))
        p = e / jnp.sum(e, axis=-1, keepdims=True)
        a_ref[:, cols] = jnp.dot(p.astype(BF16), v_ref[:, cols],
                                 preferred_element_type=F32).astype(BF16)
    hres = jnp.dot(a_ref[...], wo_ref[...], preferred_element_type=F32)
    x2 = _layer_norm(DN_ALPHA * x + hres, lng_ref[...], lnb_ref[...])
    o_ref[0, :, 0:D_MODEL] = x2

    lt = lax.dot_general(wr_ref[...], x2.astype(BF16), (((1,), (1,)), ((), ())),
                         preferred_element_type=F32) + br_ref[...]
    bucket, g_lo, g_hi = _route([lt[i:i + 1, :] for i in range(N_EXPERTS)])
    route_ref[0, 0:1, :] = bucket
    route_ref[0, 1:2, :] = g_lo
    route_ref[0, 2:3, :] = g_hi
    route_ref[0, 3:8, :] = jnp.zeros((5, bucket.shape[1]), F32)
    rt = jnp.concatenate([bucket, g_lo, g_hi, jnp.zeros((LANES - 3, bucket.shape[1]), F32)], axis=0)
    o_ref[0, :, D_MODEL:D_EXT] = rt.T


def _xattn(x, mem, wq, wk, wv, wo, lng, lnb, wr_t, br_col):
    B, S, D = x.shape
    T = TOK_TILE
    nt = S // T
    tile = pl.BlockSpec((1, T, D), lambda b, s: (b, s, 0))
    consts = (wq, wk, wv, wo, lng, lnb, wr_t, br_col)
    return pl.pallas_call(
        _xattn_kernel,
        grid=(B, nt),
        in_specs=[tile, pl.BlockSpec((1, N_MEM, D), lambda b, s: (b, 0, 0))]
        + [_const_spec(c.shape) for c in consts],
        out_specs=[pl.BlockSpec((1, T, D_EXT), lambda b, s: (b, s, 0)),
                   pl.BlockSpec((1, 8, T), lambda b, s: (b * nt + s, 0, 0))],
        out_shape=[jax.ShapeDtypeStruct((B, S, D_EXT), F32),
                   jax.ShapeDtypeStruct((B * nt, 8, T), F32)],
        scratch_shapes=[
            pltpu.VMEM((N_MEM, D), BF16),
            pltpu.VMEM((N_MEM, D), BF16),
            pltpu.VMEM((T, D), BF16),
        ],
        compiler_params=pltpu.CompilerParams(
            dimension_semantics=("arbitrary", "arbitrary"), vmem_limit_bytes=VMEM_LIMIT),
        name="xattn",
    )(x, mem, *consts)


def _moe_kernel(elo_ref, ehi_ref, nrows_ref, start_ref, order_ref, x_hbm,
                wg_lo, wu_lo, wd_lo, wg_hi, wu_hi, wd_hi, lng_ref, lnb_ref, o_hbm,
                xbuf, obuf, gsem, ssem):
    del elo_ref, ehi_ref
    i = pl.program_id(0)
    n_tiles = pl.num_programs(0)
    tm = xbuf.shape[1] * SUBLANES
    slot = lax.rem(i, 2)
    nrows = nrows_ref[i]
    nxt = jnp.minimum(i + 1, n_tiles - 1)
    has_next = jnp.logical_and(i + 1 < n_tiles, nrows_ref[nxt] > 0)

    def start_gather(tile, s):
        base = start_ref[tile]

        def body(j, c):
            for k in range(SUBLANES):
                t = order_ref[base + j * SUBLANES + k]
                pltpu.make_async_copy(x_hbm.at[pl.ds(t, 1)], xbuf.at[s, j, pl.ds(k, 1)],
                                      gsem.at[s]).start(k % 2)
            return c
        lax.fori_loop(0, tm // SUBLANES, body, 0)

    def start_scatter(s, n):
        base = start_ref[i]

        @pl.when(n == tm)
        def _():
            def body(j, c):
                for k in range(SUBLANES):
                    t = order_ref[base + j * SUBLANES + k]
                    pltpu.make_async_copy(obuf.at[s, j, pl.ds(k, 1)], o_hbm.at[pl.ds(t, 1)],
                                          ssem.at[s]).start(k % 2)
                return c
            lax.fori_loop(0, tm // SUBLANES, body, 0)

        @pl.when(n < tm)
        def _():
            def body(r, c):
                t = order_ref[base + r]
                pltpu.make_async_copy(obuf.at[s, r // SUBLANES, pl.ds(r % SUBLANES, 1)],
                                      o_hbm.at[pl.ds(t, 1)], ssem.at[s]).start()
                return c
            lax.fori_loop(0, n, body, 0)

    def wait_scatter(s, n):
        @pl.when(n == tm)
        def _():
            pltpu.make_async_copy(obuf.at[s], obuf.at[s], ssem.at[s]).wait()

        @pl.when(n < tm)
        def _():
            def body(r, c):
                pltpu.make_async_copy(obuf.at[s, 0, pl.ds(0, 1)], obuf.at[s, 0, pl.ds(0, 1)],
                                      ssem.at[s]).wait()
                return c
            lax.fori_loop(0, n, body, 0)

    @pl.when(jnp.logical_and(i == 0, nrows > 0))
    def _():
        start_gather(0, 0)

    @pl.when(nrows > 0)
    def _():
        pltpu.make_async_copy(xbuf.at[slot], xbuf.at[slot], gsem.at[slot]).wait()
        xe = xbuf[slot].reshape(tm, xbuf.shape[3])
        x = xe[:, 0:D_MODEL]
        g_lo = xe[:, COL_GLO:COL_GLO + 1]
        g_hi = xe[:, COL_GHI:COL_GHI + 1]
        xb = x.astype(BF16)

        def expert(wg, wu, wd):
            hg = jnp.dot(xb, wg[0], preferred_element_type=F32)
            hu = jnp.dot(xb, wu[0], preferred_element_type=F32)
            return jnp.dot((_silu(hg) * hu).astype(BF16), wd[0], preferred_element_type=F32)

        y = g_lo * expert(wg_lo, wu_lo, wd_lo)
        y = y + g_hi * expert(wg_hi, wu_hi, wd_hi)
        res = _layer_norm(DN_ALPHA * x + y, lng_ref[...], lnb_ref[...])
        base_n = start_ref[nxt]
        for r in range(tm):
            t = order_ref[base_n + r]
            pltpu.make_async_copy(x_hbm.at[pl.ds(t, 1)],
                                  xbuf.at[1 - slot, r // SUBLANES, pl.ds(r % SUBLANES, 1)],
                                  gsem.at[1 - slot]).start(r % 2)
        obuf[slot] = res.reshape(tm // SUBLANES, SUBLANES, D_MODEL)
        start_scatter(slot, nrows)

        @pl.when(i > 0)
        def _():
            wait_scatter(1 - slot, nrows_ref[jnp.maximum(i - 1, 0)])

        @pl.when(jnp.logical_not(has_next))
        def _():
            wait_scatter(slot, nrows)
            pltpu.make_async_copy(xbuf.at[1 - slot], xbuf.at[1 - slot], gsem.at[1 - slot]).wait()


def _moe(x2e, order_pad, wg, wu, wd, lng, lnb, tile_elo, tile_ehi, tile_nrows, tile_start):
    n_tok, de = x2e.shape
    n_tiles = tile_start.shape[0]
    tm, D = MOE_TILE, D_MODEL
    wlo = lambda s: pl.BlockSpec((1,) + s, lambda i, elo, ehi, nr, st, od: (elo[i], 0, 0))
    whi = lambda s: pl.BlockSpec((1,) + s, lambda i, elo, ehi, nr, st, od: (ehi[i], 0, 0))
    cst = lambda s: pl.BlockSpec(s, lambda i, elo, ehi, nr, st, od: (0, 0))
    gu, dn = (D, D_EXPERT), (D_EXPERT, D)
    grid_spec = pltpu.PrefetchScalarGridSpec(
        num_scalar_prefetch=5,
        grid=(n_tiles,),
        in_specs=[pl.BlockSpec(memory_space=pl.ANY),
                  wlo(gu), wlo(gu), wlo(dn), whi(gu), whi(gu), whi(dn),
                  cst(lng.shape), cst(lnb.shape)],
        out_specs=pl.BlockSpec(memory_space=pl.ANY),
        scratch_shapes=[
            pltpu.VMEM((2, tm // SUBLANES, SUBLANES, de), F32),
            pltpu.VMEM((2, tm // SUBLANES, SUBLANES, D), F32),
            pltpu.SemaphoreType.DMA((2,)),
            pltpu.SemaphoreType.DMA((2,)),
        ],
    )
    return pl.pallas_call(
        _moe_kernel,
        grid_spec=grid_spec,
        out_shape=jax.ShapeDtypeStruct((n_tok, D), F32),
        compiler_params=pltpu.CompilerParams(
            dimension_semantics=("arbitrary",), vmem_limit_bytes=VMEM_LIMIT),
        name="moe",
    )(tile_elo, tile_ehi, tile_nrows, tile_start, order_pad, x2e, wg, wu, wd, wg, wu, wd, lng, lnb)


def _dispatch_plan(bucket, n_tok):
    tm = MOE_TILE
    n_tiles = n_tok // tm + N_BUCKETS
    order = jnp.argsort(bucket, stable=True).astype(jnp.int32)
    kb = jnp.arange(N_BUCKETS, dtype=jnp.int32)
    counts = jnp.sum((bucket[:, None] == kb[None, :]).astype(jnp.int32), axis=0)
    tiles_per = (counts + tm - 1) // tm
    tile_end = jnp.cumsum(tiles_per)
    tile_start = tile_end - tiles_per
    sorted_start = jnp.cumsum(counts) - counts
    n_valid = tile_end[-1]
    ti = jnp.arange(n_tiles, dtype=jnp.int32)
    valid = ti < n_valid
    tile_bucket = jnp.sum((ti[:, None] >= tile_end[None, :]).astype(jnp.int32), axis=1)
    last_bucket = jnp.sum((n_valid - 1 >= tile_end).astype(jnp.int32))
    tile_bucket = jnp.minimum(jnp.where(valid, tile_bucket, last_bucket), N_BUCKETS - 1)
    pa = jnp.asarray([p[0] for p in PAIRS], jnp.int32)
    pb = jnp.asarray([p[1] for p in PAIRS], jnp.int32)
    grp, pair = tile_bucket // len(PAIRS), tile_bucket % len(PAIRS)
    tile_elo = grp * EXPERTS_PER_GROUP + pa[pair]
    tile_ehi = grp * EXPERTS_PER_GROUP + pb[pair]
    first_row = (ti - tile_start[tile_bucket]) * tm
    tile_nrows = jnp.where(valid, jnp.clip(counts[tile_bucket] - first_row, 0, tm), 0)
    start = jnp.clip(sorted_start[tile_bucket] + first_row, 0, n_tok)
    order_pad = jnp.concatenate([order, jnp.zeros((tm,), jnp.int32)])
    return (order_pad, tile_elo.astype(jnp.int32), tile_ehi.astype(jnp.int32),
            tile_nrows.astype(jnp.int32), start.astype(jnp.int32))


def kernel(x, mem, w_in, w_a2, b_a, gla_norm_g, w_s, b_s, sgu_ln_g, sgu_ln_b, w_out, wq_x, wk_x, wv_x,
           wo_x, w_router, b_router, w_gate, w_up, w_down, ln_g, ln_b):
    B, S, D = x.shape
    n_tok = B * S

    segs = (slice(0, 256), slice(256, 512), slice(512, 1024), slice(1024, 1536),
            slice(1552, 2064), slice(2064, 2576))
    w_in_r = jnp.concatenate(
        [w_in[:, :, s] for s in segs]
        + [jnp.pad(w_in[:, :, 1536:1552], ((0, 0), (0, 0), (0, A_PAD - GATE_RANK)))], axis=-1).astype(BF16)
    w_a2_p = jnp.pad(w_a2, ((0, 0), (0, A_PAD - GATE_RANK), (0, 0))).astype(BF16)
    bs_full = jnp.broadcast_to(b_s[:, :, :, None], b_s.shape + (SGU_CH,))
    w_out_b, wq_b, wk_b, wv_b, wo_b = (w.astype(BF16) for w in (w_out, wq_x, wk_x, wv_x, wo_x))
    wg_b, wu_b, wd_b = (w.astype(BF16) for w in (w_gate, w_up, w_down))
    wr_t = w_router.T.astype(BF16)
    br_col = jnp.broadcast_to(b_router[:, None], (N_EXPERTS, TOK_TILE))
    row = lambda v: v.reshape(1, -1)

    for l in range(DEPTH):
        x = _mixer(x, w_in_r[l], w_a2_p[l], row(b_a[l]), row(gla_norm_g[l]), w_s[l], bs_full[l],
                   row(sgu_ln_g[l]), row(sgu_ln_b[l]), w_out_b[l], row(ln_g[l, 0]), row(ln_b[l, 0]))
        x2e, route = _xattn(x, mem, wq_b[l], wk_b[l], wv_b[l], wo_b[l], row(ln_g[l, 1]), row(ln_b[l, 1]),
                            wr_t, br_col)
        bucket = route[:, 0, :].reshape(-1).astype(jnp.int32)
        order_pad, t_elo, t_ehi, t_nrows, t_start = _dispatch_plan(bucket, n_tok)
        x = _moe(x2e.reshape(n_tok, D_EXT), order_pad, wg_b[l], wu_b[l], wd_b[l],
                 row(ln_g[l, 2]), row(ln_b[l, 2]), t_elo, t_ehi, t_nrows, t_start).reshape(B, S, D)
    return x
```

```python
import functools

import jax
import jax.numpy as jnp
from jax import lax
from jax.experimental import pallas as pl
from jax.experimental.pallas import tpu as pltpu

F32 = jnp.float32
BF16 = jnp.bfloat16

D_MODEL = 1024
DEPTH = 2
CHUNK = 64
N_MEM = 256
D_GLA = 512
D_SGU = 512
GLA_HEADS = 4
D_QK = 256
GLA_DK = 64
GLA_DV = 128
GATE_RANK = 16
GATE_TEMP = 16.0
SGU_GROUPS = 4
SGU_BLOCK = 128
SGU_CH = 128
XATTN_HEADS = 4
XATTN_DH = 256
N_EXPERTS = 16
N_EXPERT_GROUPS = 4
EXPERTS_PER_GROUP = 4
D_EXPERT = 512
DN_ALPHA = (2.0 * DEPTH) ** 0.25
LN_EPS = 1e-5
RMS_EPS = 1e-6

LANES = 128
SUBLANES = 8
A_PAD = LANES
C_Q, C_K, C_V, C_R = 0, 256, 512, 1024
C_U, C_VS, C_A = 1536, 2048, 2560
D_INP = C_A + A_PAD
D_EXT = D_MODEL + LANES
COL_GLO, COL_GHI = D_MODEL + 1, D_MODEL + 2

PAIRS = ((0, 1), (0, 2), (0, 3), (1, 2), (1, 3), (2, 3))
N_BUCKETS = N_EXPERT_GROUPS * len(PAIRS)

TOK_TILE = 512
MOE_TILE = 512
VMEM_LIMIT = 56 * 1024 * 1024


def _layer_norm(x, g, b):
    mu = jnp.mean(x, axis=-1, keepdims=True)
    xc = x - mu
    var = jnp.mean(xc * xc, axis=-1, keepdims=True)
    return xc * lax.rsqrt(var + LN_EPS) * g + b


def _gelu_tanh(x):
    return 0.5 * x * (1.0 + jnp.tanh(0.7978845608028654 * (x + 0.044715 * (x * x * x))))


def _silu(x):
    return x / (1.0 + jnp.exp(-x))


def _mixer_kernel(x_ref, xn_ref, win_ref, wa2_ref, ba_ref, gng_ref, ws_ref, bs_ref, slg_ref, slb_ref,
                  wout_ref, lng_ref, lnb_ref, o_ref, st_ref, dec_ref, upd_ref, snap_ref, p0_ref, p1_ref,
                  g0_ref, g1_ref, y0_ref, y1_ref, *, steps_per_seq):
    T = xn_ref.shape[0]
    g = pl.program_id(0)
    consts = (wa2_ref, ba_ref, gng_ref, ws_ref, bs_ref, slg_ref, slb_ref, wout_ref, lng_ref, lnb_ref)
    gla_scratch = (st_ref, dec_ref, upd_ref, snap_ref)

    @pl.when(lax.rem(g, steps_per_seq) == 0)
    def _():
        st_ref[...] = jnp.zeros_like(st_ref)

    @pl.when(g == 0)
    def _():
        p0_ref[...] = jnp.dot(x_ref[pl.ds(0, T), :].astype(BF16), win_ref[...], preferred_element_type=F32)

    def in_proj(x):
        return jnp.dot(x.astype(BF16), win_ref[...], preferred_element_type=F32)

    lo, hi = pl.ds(0, T), pl.ds(T, T)
    p1_ref[...] = in_proj(x_ref[hi, :])
    o_ref[lo, :] = _mixer_gate_stage(x_ref.at[lo, :], p0_ref, *gla_scratch, g0_ref, y0_ref, *consts)
    p0_ref[...] = in_proj(xn_ref[...])
    o_ref[hi, :] = _mixer_gate_stage(x_ref.at[hi, :], p1_ref, *gla_scratch, g1_ref, y1_ref, *consts)


def _mixer_gate_stage(x_ref, p_ref, st_ref, dec_ref, upd_ref, snap_ref, g_ref, y_ref, wa2_ref, ba_ref, gng_ref,
                      ws_ref, bs_ref, slg_ref, slb_ref, wout_ref, lng_ref, lnb_ref):
    T = p_ref.shape[0]

    z = jnp.dot(p_ref[:, C_A:C_A + A_PAD].astype(BF16), wa2_ref[...],
                preferred_element_type=F32) + ba_ref[...]
    g_ref[...] = (jnp.minimum(z, 0.0) - jnp.log1p(jnp.exp(-jnp.abs(z)))) * (1.0 / GATE_TEMP)

    ri = lax.broadcasted_iota(jnp.int32, (CHUNK, 3 * CHUNK), 0)
    ci = lax.broadcasted_iota(jnp.int32, (CHUNK, 3 * CHUNK), 1) % CHUNK
    tri3 = (ci <= ri).astype(BF16)

    def cumsum_chunk(gc):
        hi = gc.astype(BF16)
        r1 = gc - hi.astype(F32)
        mid = r1.astype(BF16)
        lo = (r1 - mid.astype(F32)).astype(BF16)
        return jnp.dot(tri3, jnp.concatenate([hi, mid, lo], axis=0), preferred_element_type=F32)
    hr = lax.broadcasted_iota(jnp.int32, (D_GLA, D_QK), 0) // GLA_DV
    hc = lax.broadcasted_iota(jnp.int32, (D_GLA, D_QK), 1) // GLA_DK
    head_mask = (hr == hc).astype(F32)

    n_chunks = T // CHUNK
    for c in range(n_chunks):
        rows = pl.ds(c * CHUNK, CHUNK)
        bcum = cumsum_chunk(g_ref[rows, :])
        b_last = bcum[CHUNK - 1:CHUNK, :]
        dec_ref[c] = jnp.exp(b_last)
        k_dec = p_ref[rows, C_K:C_K + D_QK] * jnp.exp(b_last - bcum)
        v_c = p_ref[rows, C_V:C_V + D_GLA]
        upd = lax.dot_general(v_c.astype(BF16), k_dec.astype(BF16), (((0,), (0,)), ((), ())),
                              preferred_element_type=F32)
        upd_ref[c] = upd * head_mask
    for c in range(n_chunks):
        st = st_ref[...] * dec_ref[c] + upd_ref[c]
        st_ref[...] = st
        snap_ref[c] = st.astype(BF16)
    for c in range(n_chunks):
        rows = pl.ds(c * CHUNK, CHUNK)
        q_c = p_ref[rows, C_Q:C_Q + D_QK] * (GLA_DK ** -0.5)
        o_c = lax.dot_general(q_c.astype(BF16), snap_ref[c], (((1,), (1,)), ((), ())),
                              preferred_element_type=F32)
        r_c = p_ref[rows, C_R:C_R + D_GLA]
        for h in range(GLA_HEADS):
            cols = slice(h * GLA_DV, (h + 1) * GLA_DV)
            oh = o_c[:, cols]
            oh = oh * lax.rsqrt(jnp.mean(oh * oh, axis=-1, keepdims=True) + RMS_EPS)
            y_ref[rows, cols] = (oh * gng_ref[:, cols] * _silu(r_c[:, cols])).astype(BF16)

    ti = lax.broadcasted_iota(jnp.int32, (SGU_BLOCK, SGU_BLOCK), 0) // CHUNK
    si = lax.broadcasted_iota(jnp.int32, (SGU_BLOCK, SGU_BLOCK), 1) // CHUNK
    causal = si <= ti
    for gi in range(SGU_GROUPS):
        w_m = jnp.where(causal, ws_ref[gi], 0.0).astype(BF16)
        cu = slice(C_U + gi * SGU_CH, C_U + (gi + 1) * SGU_CH)
        cv = slice(C_VS + gi * SGU_CH, C_VS + (gi + 1) * SGU_CH)
        cg = slice(gi * SGU_CH, (gi + 1) * SGU_CH)
        n_blocks = T // SGU_BLOCK
        vn = [_layer_norm(_gelu_tanh(p_ref[pl.ds(n * SGU_BLOCK, SGU_BLOCK), cv]),
                          slg_ref[:, cg], slb_ref[:, cg]).astype(BF16) for n in range(n_blocks)]
        mixed = jnp.dot(w_m, jnp.concatenate(vn, axis=1), preferred_element_type=F32)
        for n in range(n_blocks):
            rows = pl.ds(n * SGU_BLOCK, SGU_BLOCK)
            u = _gelu_tanh(p_ref[rows, cu])
            m_n = mixed[:, n * SGU_CH:(n + 1) * SGU_CH] + bs_ref[gi]
            y_ref[rows, D_GLA + gi * SGU_CH:D_GLA + (gi + 1) * SGU_CH] = (u * m_n).astype(BF16)

    h = jnp.dot(y_ref[...], wout_ref[...], preferred_element_type=F32)
    return _layer_norm(DN_ALPHA * x_ref[...] + h, lng_ref[...], lnb_ref[...])


def _const_spec(shape, single_buffer=False):
    nd = len(shape)
    if single_buffer:
        return pl.BlockSpec(shape, lambda *_: (0,) * nd, pipeline_mode=pl.Buffered(1))
    return pl.BlockSpec(shape, lambda *_: (0,) * nd)


def _mixer(x, win, wa2, ba, gng, ws, bs_full, slg, slb, wout, lng, lnb):
    B, S, D = x.shape
    T = TOK_TILE
    n_tiles = B * S // T
    assert S % (2 * T) == 0
    x2d = x.reshape(B * S, D)
    consts = (win, wa2, ba, gng, ws, bs_full, slg, slb, wout, lng, lnb)
    pair = pl.BlockSpec((2 * T, D), lambda g: (g, 0))
    nxt = pl.BlockSpec((T, D), lambda g: (jnp.minimum(2 * g + 2, n_tiles - 1), 0))
    out = pl.pallas_call(
        functools.partial(_mixer_kernel, steps_per_seq=S // (2 * T)),
        grid=(n_tiles // 2,),
        in_specs=[pair, nxt] + [_const_spec(c.shape, single_buffer=True) for c in consts],
        out_specs=pair,
        out_shape=jax.ShapeDtypeStruct((B * S, D), F32),
        scratch_shapes=[
            pltpu.VMEM((D_GLA, D_QK), F32),
            pltpu.VMEM((T // CHUNK, 1, D_QK), F32),
            pltpu.VMEM((T // CHUNK, D_GLA, D_QK), F32),
            pltpu.VMEM((T // CHUNK, D_GLA, D_QK), BF16),
            pltpu.VMEM((T, D_INP), F32),
            pltpu.VMEM((T, D_INP), F32),
            pltpu.VMEM((T, D_QK), F32),
            pltpu.VMEM((T, D_QK), F32),
            pltpu.VMEM((T, D), BF16),
            pltpu.VMEM((T, D), BF16),
        ],
        compiler_params=pltpu.CompilerParams(
            dimension_semantics=("arbitrary",), vmem_limit_bytes=VMEM_LIMIT),
        name="mixer",
    )(x2d, x2d, *consts)
    return out.reshape(B, S, D)


def _route(logits):
    m = functools.reduce(jnp.maximum, logits)
    e = [jnp.exp(l - m) for l in logits]
    inv = 1.0 / functools.reduce(jnp.add, e)
    sc = [v * inv for v in e]

    def top2(a):
        first = functools.reduce(jnp.maximum, a)
        second = None
        for i in range(len(a)):
            for j in range(i + 1, len(a)):
                mn = jnp.minimum(a[i], a[j])
                second = mn if second is None else jnp.maximum(second, mn)
        return first + second

    gs = [top2(sc[g * EXPERTS_PER_GROUP:(g + 1) * EXPERTS_PER_GROUP]) for g in range(N_EXPERT_GROUPS)]
    best, g_sel = gs[0], jnp.zeros_like(gs[0])
    for g in range(1, N_EXPERT_GROUPS):
        better = gs[g] > best
        g_sel = jnp.where(better, float(g), g_sel)
        best = jnp.where(better, gs[g], best)
    a = []
    for j in range(EXPERTS_PER_GROUP):
        v = jnp.zeros_like(best)
        for g in range(N_EXPERT_GROUPS):
            v = v + jnp.where(g_sel == float(g), sc[g * EXPERTS_PER_GROUP + j], 0.0)
        a.append(v)
    w1, i1 = a[0], jnp.zeros_like(a[0])
    for j in range(1, EXPERTS_PER_GROUP):
        better = a[j] > w1
        i1 = jnp.where(better, float(j), i1)
        w1 = jnp.where(better, a[j], w1)
    w2, i2 = None, None
    for j in range(EXPERTS_PER_GROUP):
        cand = jnp.where(i1 == float(j), -1.0, a[j])
        if w2 is None:
            w2, i2 = cand, jnp.zeros_like(cand)
        else:
            better = cand > w2
            i2 = jnp.where(better, float(j), i2)
            w2 = jnp.where(better, cand, w2)
    tot = w1 + w2
    w1n, w2n = w1 / tot, w2 / tot
    first_is_lo = i1 < i2
    lo = jnp.where(first_is_lo, i1, i2)
    hi = jnp.where(first_is_lo, i2, i1)
    pair = jnp.zeros_like(lo)
    for pi, (pa, pb) in enumerate(PAIRS):
        pair = jnp.where((lo == float(pa)) & (hi == float(pb)), float(pi), pair)
    bucket = g_sel * float(len(PAIRS)) + pair
    g_lo = jnp.where(first_is_lo, w1n, w2n)
    g_hi = jnp.where(first_is_lo, w2n, w1n)
    return bucket, g_lo, g_hi


def _xattn_kernel(x_ref, mem_ref, wq_ref, wk_ref, wv_ref, wo_ref, lng_ref, lnb_ref, wr_ref, br_ref,
                  o_ref, route_ref, k_ref, v_ref, a_ref):
    @pl.when(pl.program_id(1) == 0)
    def _():
        mb = mem_ref[0].astype(BF16)
        k_ref[...] = jnp.dot(mb, wk_ref[...], preferred_element_type=F32).astype(BF16)
        v_ref[...] = jnp.dot(mb, wv_ref[...], preferred_element_type=F32).astype(BF16)

    x = x_ref[0]
    q = jnp.dot(x.astype(BF16), wq_ref[...], preferred_element_type=F32).astype(BF16)
    for h in range(XATTN_HEADS):
        cols = slice(h * XATTN_DH, (h + 1) * XATTN_DH)
        s = lax.dot_general(q[:, cols], k_ref[:, cols], (((1,), (1,)), ((), ())),
                            preferred_element_type=F32) * (XATTN_DH ** -0.5)
        e = jnp.exp(s - jnp.max(s, axis=-1, keepdims=True))
        p = e / jnp.sum(e, axis=-1, keepdims=True)
        a_ref[:, cols] = jnp.dot(p.astype(BF16), v_ref[:, cols],
                                 preferred_element_type=F32).astype(BF16)
    hres = jnp.dot(a_ref[...], wo_ref[...], preferred_element_type=F32)
    x2 = _layer_norm(DN_ALPHA * x + hres, lng_ref[...], lnb_ref[...])
    o_ref[0, :, 0:D_MODEL] = x2

    lt = lax.dot_general(wr_ref[...], x2.astype(BF16), (((1,), (1,)), ((), ())),
                         preferred_element_type=F32) + br_ref[...]
    bucket, g_lo, g_hi = _route([lt[i:i + 1, :] for i in range(N_EXPERTS)])
    route_ref[0, 0:1, :] = bucket
    route_ref[0, 1:2, :] = g_lo
    route_ref[0, 2:3, :] = g_hi
    route_ref[0, 3:8, :] = jnp.zeros((5, bucket.shape[1]), F32)
    rt = jnp.concatenate([bucket, g_lo, g_hi, jnp.zeros((LANES - 3, bucket.shape[1]), F32)], axis=0)
    o_ref[0, :, D_MODEL:D_EXT] = rt.T


def _xattn(x, mem, wq, wk, wv, wo, lng, lnb, wr_t, br_col):
    B, S, D = x.shape
    T = TOK_TILE
    nt = S // T
    tile = pl.BlockSpec((1, T, D), lambda b, s: (b, s, 0))
    consts = (wq, wk, wv, wo, lng, lnb, wr_t, br_col)
    return pl.pallas_call(
        _xattn_kernel,
        grid=(B, nt),
        in_specs=[tile, pl.BlockSpec((1, N_MEM, D), lambda b, s: (b, 0, 0))]
        + [_const_spec(c.shape) for c in consts],
        out_specs=[pl.BlockSpec((1, T, D_EXT), lambda b, s: (b, s, 0)),
                   pl.BlockSpec((1, 8, T), lambda b, s: (b * nt + s, 0, 0))],
        out_shape=[jax.ShapeDtypeStruct((B, S, D_EXT), F32),
                   jax.ShapeDtypeStruct((B * nt, 8, T), F32)],
        scratch_shapes=[
            pltpu.VMEM((N_MEM, D), BF16),
            pltpu.VMEM((N_MEM, D), BF16),
            pltpu.VMEM((T, D), BF16),
        ],
        compiler_params=pltpu.CompilerParams(
            dimension_semantics=("arbitrary", "arbitrary"), vmem_limit_bytes=VMEM_LIMIT),
        name="xattn",
    )(x, mem, *consts)


def _moe_kernel(elo_ref, ehi_ref, nrows_ref, start_ref, order_ref, x_hbm,
                wg_lo, wu_lo, wd_lo, wg_hi, wu_hi, wd_hi, lng_ref, lnb_ref, o_hbm,
                xbuf, obuf, gsem, ssem):
    del elo_ref, ehi_ref
    i = pl.program_id(0)
    n_tiles = pl.num_programs(0)
    tm = xbuf.shape[0] * SUBLANES
    nrows = nrows_ref[i]
    nxt = jnp.minimum(i + 1, n_tiles - 1)
    has_next = jnp.logical_and(i + 1 < n_tiles, nrows_ref[nxt] > 0)

    def gather_row(base, j, k):
        t = order_ref[base + j * SUBLANES + k]
        pltpu.make_async_copy(x_hbm.at[pl.ds(t, 1)], xbuf.at[j, pl.ds(k, 1)], gsem.at[0]).start(k % 2)

    def scatter_row(base, j, k):
        t = order_ref[base + j * SUBLANES + k]
        pltpu.make_async_copy(obuf.at[j, pl.ds(k, 1)], o_hbm.at[pl.ds(t, 1)], ssem.at[0]).start(k % 2)

    def wait_gather():
        pltpu.make_async_copy(xbuf, xbuf, gsem.at[0]).wait()

    def wait_scatter(n):
        @pl.when(n == tm)
        def _():
            pltpu.make_async_copy(obuf, obuf, ssem.at[0]).wait()

        @pl.when(n < tm)
        def _():
            def body(r, c):
                pltpu.make_async_copy(obuf.at[0, pl.ds(0, 1)], obuf.at[0, pl.ds(0, 1)], ssem.at[0]).wait()
                return c
            lax.fori_loop(0, n, body, 0)

    @pl.when(jnp.logical_and(i == 0, nrows > 0))
    def _():
        base0 = start_ref[0]

        def body(j, c):
            for k in range(SUBLANES):
                gather_row(base0, j, k)
            return c
        lax.fori_loop(0, tm // SUBLANES, body, 0)

    @pl.when(nrows > 0)
    def _():
        wait_gather()

        @pl.when(i > 0)
        def _():
            wait_scatter(nrows_ref[jnp.maximum(i - 1, 0)])

        xe = xbuf[...].reshape(tm, xbuf.shape[2])
        x = xe[:, 0:D_MODEL]
        g_lo = xe[:, COL_GLO:COL_GLO + 1]
        g_hi = xe[:, COL_GHI:COL_GHI + 1]
        xb = x.astype(BF16)

        def expert(wg, wu, wd):
            hg = jnp.dot(xb, wg[0], preferred_element_type=F32)
            hu = jnp.dot(xb, wu[0], preferred_element_type=F32)
            return jnp.dot((_silu(hg) * hu).astype(BF16), wd[0], preferred_element_type=F32)

        y = g_lo * expert(wg_lo, wu_lo, wd_lo)
        y = y + g_hi * expert(wg_hi, wu_hi, wd_hi)
        res = _layer_norm(DN_ALPHA * x + y, lng_ref[...], lnb_ref[...])
        base_n = start_ref[nxt]
        for r in range(tm):
            gather_row(base_n, r // SUBLANES, r % SUBLANES)
        obuf[...] = res.reshape(tm // SUBLANES, SUBLANES, D_MODEL)

        base = start_ref[i]

        @pl.when(nrows == tm)
        def _():
            for r in range(tm):
                scatter_row(base, r // SUBLANES, r % SUBLANES)

        @pl.when(nrows < tm)
        def _():
            def body(r, c):
                t = order_ref[base + r]
                pltpu.make_async_copy(obuf.at[r // SUBLANES, pl.ds(r % SUBLANES, 1)],
                                      o_hbm.at[pl.ds(t, 1)], ssem.at[0]).start()
                return c
            lax.fori_loop(0, nrows, body, 0)

        @pl.when(jnp.logical_not(has_next))
        def _():
            wait_scatter(nrows)
            wait_gather()


def _moe(x2e, order_pad, wg, wu, wd, lng, lnb, tile_elo, tile_ehi, tile_nrows, tile_start):
    n_tok, de = x2e.shape
    n_tiles = tile_start.shape[0]
    tm, D = MOE_TILE, D_MODEL
    wlo = lambda s: pl.BlockSpec((1,) + s, lambda i, elo, ehi, nr, st, od: (elo[i], 0, 0))
    whi = lambda s: pl.BlockSpec((1,) + s, lambda i, elo, ehi, nr, st, od: (ehi[i], 0, 0))
    cst = lambda s: pl.BlockSpec(s, lambda i, elo, ehi, nr, st, od: (0, 0))
    gu, dn = (D, D_EXPERT), (D_EXPERT, D)
    grid_spec = pltpu.PrefetchScalarGridSpec(
        num_scalar_prefetch=5,
        grid=(n_tiles,),
        in_specs=[pl.BlockSpec(memory_space=pl.ANY),
                  wlo(gu), wlo(gu), wlo(dn), whi(gu), whi(gu), whi(dn),
                  cst(lng.shape), cst(lnb.shape)],
        out_specs=pl.BlockSpec(memory_space=pl.ANY),
        scratch_shapes=[
            pltpu.VMEM((tm // SUBLANES, SUBLANES, de), F32),
            pltpu.VMEM((tm // SUBLANES, SUBLANES, D), F32),
            pltpu.SemaphoreType.DMA((1,)),
            pltpu.SemaphoreType.DMA((1,)),
        ],
    )
    return pl.pallas_call(
        _moe_kernel,
        grid_spec=grid_spec,
        out_shape=jax.ShapeDtypeStruct((n_tok, D), F32),
        compiler_params=pltpu.CompilerParams(
            dimension_semantics=("arbitrary",), vmem_limit_bytes=VMEM_LIMIT),
        name="moe",
    )(tile_elo, tile_ehi, tile_nrows, tile_start, order_pad, x2e, wg, wu, wd, wg, wu, wd, lng, lnb)


def _dispatch_plan(bucket, n_tok):
    tm = MOE_TILE
    n_tiles = n_tok // tm + N_BUCKETS
    order = jnp.argsort(bucket, stable=True).astype(jnp.int32)
    kb = jnp.arange(N_BUCKETS, dtype=jnp.int32)
    counts = jnp.sum((bucket[:, None] == kb[None, :]).astype(jnp.int32), axis=0)
    tiles_per = (counts + tm - 1) // tm
    tile_end = jnp.cumsum(tiles_per)
    tile_start = tile_end - tiles_per
    sorted_start = jnp.cumsum(counts) - counts
    n_valid = tile_end[-1]
    ti = jnp.arange(n_tiles, dtype=jnp.int32)
    valid = ti < n_valid
    tile_bucket = jnp.sum((ti[:, None] >= tile_end[None, :]).astype(jnp.int32), axis=1)
    last_bucket = jnp.sum((n_valid - 1 >= tile_end).astype(jnp.int32))
    tile_bucket = jnp.minimum(jnp.where(valid, tile_bucket, last_bucket), N_BUCKETS - 1)
    pa = jnp.asarray([p[0] for p in PAIRS], jnp.int32)
    pb = jnp.asarray([p[1] for p in PAIRS], jnp.int32)
    grp, pair = tile_bucket // len(PAIRS), tile_bucket % len(PAIRS)
    tile_elo = grp * EXPERTS_PER_GROUP + pa[pair]
    tile_ehi = grp * EXPERTS_PER_GROUP + pb[pair]
    first_row = (ti - tile_start[tile_bucket]) * tm
    tile_nrows = jnp.where(valid, jnp.clip(counts[tile_bucket] - first_row, 0, tm), 0)
    start = jnp.clip(sorted_start[tile_bucket] + first_row, 0, n_tok)
    order_pad = jnp.concatenate([order, jnp.zeros((tm,), jnp.int32)])
    return (order_pad, tile_elo.astype(jnp.int32), tile_ehi.astype(jnp.int32),
            tile_nrows.astype(jnp.int32), start.astype(jnp.int32))


def kernel(x, mem, w_in, w_a2, b_a, gla_norm_g, w_s, b_s, sgu_ln_g, sgu_ln_b, w_out, wq_x, wk_x, wv_x,
           wo_x, w_router, b_router, w_gate, w_up, w_down, ln_g, ln_b):
    B, S, D = x.shape
    n_tok = B * S

    segs = (slice(0, 256), slice(256, 512), slice(512, 1024), slice(1024, 1536),
            slice(1552, 2064), slice(2064, 2576))
    w_in_r = jnp.concatenate(
        [w_in[:, :, s] for s in segs]
        + [jnp.pad(w_in[:, :, 1536:1552], ((0, 0), (0, 0), (0, A_PAD - GATE_RANK)))], axis=-1).astype(BF16)
    w_a2_p = jnp.pad(w_a2, ((0, 0), (0, A_PAD - GATE_RANK), (0, 0))).astype(BF16)
    bs_full = jnp.broadcast_to(b_s[:, :, :, None], b_s.shape + (SGU_CH,))
    w_out_b, wq_b, wk_b, wv_b, wo_b = (w.astype(BF16) for w in (w_out, wq_x, wk_x, wv_x, wo_x))
    wg_b, wu_b, wd_b = (w.astype(BF16) for w in (w_gate, w_up, w_down))
    wr_t = w_router.T.astype(BF16)
    br_col = jnp.broadcast_to(b_router[:, None], (N_EXPERTS, TOK_TILE))
    row = lambda v: v.reshape(1, -1)

    for l in range(DEPTH):
        x = _mixer(x, w_in_r[l], w_a2_p[l], row(b_a[l]), row(gla_norm_g[l]), w_s[l], bs_full[l],
                   row(sgu_ln_g[l]), row(sgu_ln_b[l]), w_out_b[l], row(ln_g[l, 0]), row(ln_b[l, 0]))
        x2e, route = _xattn(x, mem, wq_b[l], wk_b[l], wv_b[l], wo_b[l], row(ln_g[l, 1]), row(ln_b[l, 1]),
                            wr_t, br_col)
        bucket = route[:, 0, :].reshape(-1).astype(jnp.int32)
        order_pad, t_elo, t_ehi, t_nrows, t_start = _dispatch_plan(bucket, n_tok)
        x = _moe(x2e.reshape(n_tok, D_EXT), order_pad, wg_b[l], wu_b[l], wd_b[l],
                 row(ln_g[l, 2]), row(ln_b[l, 2]), t_elo, t_ehi, t_nrows, t_start).reshape(B, S, D)
    return x
```

```python
import functools

import jax
import jax.numpy as jnp
from jax import lax
from jax.experimental import pallas as pl
from jax.experimental.pallas import tpu as pltpu

F32 = jnp.float32
BF16 = jnp.bfloat16

D_MODEL = 1024
DEPTH = 2
CHUNK = 64
N_MEM = 256
D_GLA = 512
D_SGU = 512
GLA_HEADS = 4
D_QK = 256
GLA_DK = 64
GLA_DV = 128
GATE_RANK = 16
GATE_TEMP = 16.0
SGU_GROUPS = 4
SGU_BLOCK = 128
SGU_CH = 128
XATTN_HEADS = 4
XATTN_DH = 256
N_EXPERTS = 16
N_EXPERT_GROUPS = 4
EXPERTS_PER_GROUP = 4
D_EXPERT = 512
DN_ALPHA = (2.0 * DEPTH) ** 0.25
LN_EPS = 1e-5
RMS_EPS = 1e-6

LANES = 128
SUBLANES = 8
A_PAD = LANES
C_Q, C_K, C_V, C_R = 0, 256, 512, 1024
C_U, C_VS, C_A = 1536, 2048, 2560
D_INP = C_A + A_PAD
D_EXT = D_MODEL + LANES
COL_GLO, COL_GHI = D_MODEL + 1, D_MODEL + 2

PAIRS = ((0, 1), (0, 2), (0, 3), (1, 2), (1, 3), (2, 3))
N_BUCKETS = N_EXPERT_GROUPS * len(PAIRS)

TOK_TILE = 512
MOE_TILE = 512
VMEM_LIMIT = 56 * 1024 * 1024


def _layer_norm(x, g, b):
    mu = jnp.mean(x, axis=-1, keepdims=True)
    xc = x - mu
    var = jnp.mean(xc * xc, axis=-1, keepdims=True)
    return xc * lax.rsqrt(var + LN_EPS) * g + b


def _gelu_tanh(x):
    return 0.5 * x * (1.0 + jnp.tanh(0.7978845608028654 * (x + 0.044715 * (x * x * x))))


def _silu(x):
    return x / (1.0 + jnp.exp(-x))


def _mixer_kernel(x_ref, xn_ref, win_ref, wa2_ref, ba_ref, gng_ref, ws_ref, bs_ref, slg_ref, slb_ref,
                  wout_ref, lng_ref, lnb_ref, o_ref, st_ref, dec_ref, upd_ref, snap_ref, p0_ref, p1_ref,
                  g0_ref, g1_ref, y0_ref, y1_ref, *, steps_per_seq):
    T = xn_ref.shape[0]
    g = pl.program_id(0)
    consts = (wa2_ref, ba_ref, gng_ref, ws_ref, bs_ref, slg_ref, slb_ref, wout_ref, lng_ref, lnb_ref)
    gla_scratch = (st_ref, dec_ref, upd_ref, snap_ref)

    @pl.when(lax.rem(g, steps_per_seq) == 0)
    def _():
        st_ref[...] = jnp.zeros_like(st_ref)

    @pl.when(g == 0)
    def _():
        p0_ref[...] = jnp.dot(x_ref[pl.ds(0, T), :].astype(BF16), win_ref[...], preferred_element_type=F32)

    def in_proj(x):
        return jnp.dot(x.astype(BF16), win_ref[...], preferred_element_type=F32)

    lo, hi = pl.ds(0, T), pl.ds(T, T)
    p1_ref[...] = in_proj(x_ref[hi, :])
    o_ref[lo, :] = _mixer_gate_stage(x_ref.at[lo, :], p0_ref, *gla_scratch, g0_ref, y0_ref, *consts)
    p0_ref[...] = in_proj(xn_ref[...])
    o_ref[hi, :] = _mixer_gate_stage(x_ref.at[hi, :], p1_ref, *gla_scratch, g1_ref, y1_ref, *consts)


def _mixer_gate_stage(x_ref, p_ref, st_ref, dec_ref, upd_ref, snap_ref, g_ref, y_ref, wa2_ref, ba_ref, gng_ref,
                      ws_ref, bs_ref, slg_ref, slb_ref, wout_ref, lng_ref, lnb_ref):
    T = p_ref.shape[0]

    z = jnp.dot(p_ref[:, C_A:C_A + A_PAD].astype(BF16), wa2_ref[...],
                preferred_element_type=F32) + ba_ref[...]
    g_ref[...] = (jnp.minimum(z, 0.0) - jnp.log1p(jnp.exp(-jnp.abs(z)))) * (1.0 / GATE_TEMP)

    ri = lax.broadcasted_iota(jnp.int32, (CHUNK, 3 * CHUNK), 0)
    ci = lax.broadcasted_iota(jnp.int32, (CHUNK, 3 * CHUNK), 1) % CHUNK
    tri3 = (ci <= ri).astype(BF16)

    def cumsum_chunk(gc):
        hi = gc.astype(BF16)
        r1 = gc - hi.astype(F32)
        mid = r1.astype(BF16)
        lo = (r1 - mid.astype(F32)).astype(BF16)
        return jnp.dot(tri3, jnp.concatenate([hi, mid, lo], axis=0), preferred_element_type=F32)
    hr = lax.broadcasted_iota(jnp.int32, (D_GLA, D_QK), 0) // GLA_DV
    hc = lax.broadcasted_iota(jnp.int32, (D_GLA, D_QK), 1) // GLA_DK
    head_mask = (hr == hc).astype(F32)

    n_chunks = T // CHUNK
    for c in range(n_chunks):
        rows = pl.ds(c * CHUNK, CHUNK)
        bcum = cumsum_chunk(g_ref[rows, :])
        b_last = bcum[CHUNK - 1:CHUNK, :]
        dec_ref[c] = jnp.exp(b_last)
        k_dec = p_ref[rows, C_K:C_K + D_QK] * jnp.exp(b_last - bcum)
        v_c = p_ref[rows, C_V:C_V + D_GLA]
        upd = lax.dot_general(v_c.astype(BF16), k_dec.astype(BF16), (((0,), (0,)), ((), ())),
                              preferred_element_type=F32)
        upd_ref[c] = upd * head_mask
    for c in range(n_chunks):
        st = st_ref[...] * dec_ref[c] + upd_ref[c]
        st_ref[...] = st
        snap_ref[c] = st.astype(BF16)
    for c in range(n_chunks):
        rows = pl.ds(c * CHUNK, CHUNK)
        q_c = p_ref[rows, C_Q:C_Q + D_QK] * (GLA_DK ** -0.5)
        o_c = lax.dot_general(q_c.astype(BF16), snap_ref[c], (((1,), (1,)), ((), ())),
                              preferred_element_type=F32)
        r_c = p_ref[rows, C_R:C_R + D_GLA]
        for h in range(GLA_HEADS):
            cols = slice(h * GLA_DV, (h + 1) * GLA_DV)
            oh = o_c[:, cols]
            oh = oh * lax.rsqrt(jnp.mean(oh * oh, axis=-1, keepdims=True) + RMS_EPS)
            y_ref[rows, cols] = (oh * gng_ref[:, cols] * _silu(r_c[:, cols])).astype(BF16)

    ti = lax.broadcasted_iota(jnp.int32, (SGU_BLOCK, SGU_BLOCK), 0) // CHUNK
    si = lax.broadcasted_iota(jnp.int32, (SGU_BLOCK, SGU_BLOCK), 1) // CHUNK
    causal = si <= ti
    for gi in range(SGU_GROUPS):
        w_m = jnp.where(causal, ws_ref[gi], 0.0).astype(BF16)
        cu = slice(C_U + gi * SGU_CH, C_U + (gi + 1) * SGU_CH)
        cv = slice(C_VS + gi * SGU_CH, C_VS + (gi + 1) * SGU_CH)
        cg = slice(gi * SGU_CH, (gi + 1) * SGU_CH)
        n_blocks = T // SGU_BLOCK
        vn = [_layer_norm(_gelu_tanh(p_ref[pl.ds(n * SGU_BLOCK, SGU_BLOCK), cv]),
                          slg_ref[:, cg], slb_ref[:, cg]).astype(BF16) for n in range(n_blocks)]
        mixed = jnp.dot(w_m, jnp.concatenate(vn, axis=1), preferred_element_type=F32)
        for n in range(n_blocks):
            rows = pl.ds(n * SGU_BLOCK, SGU_BLOCK)
            u = _gelu_tanh(p_ref[rows, cu])
            m_n = mixed[:, n * SGU_CH:(n + 1) * SGU_CH] + bs_ref[gi]
            y_ref[rows, D_GLA + gi * SGU_CH:D_GLA + (gi + 1) * SGU_CH] = (u * m_n).astype(BF16)

    h = jnp.dot(y_ref[...], wout_ref[...], preferred_element_type=F32)
    return _layer_norm(DN_ALPHA * x_ref[...] + h, lng_ref[...], lnb_ref[...])


def _const_spec(shape, single_buffer=False):
    nd = len(shape)
    if single_buffer:
        return pl.BlockSpec(shape, lambda *_: (0,) * nd, pipeline_mode=pl.Buffered(1))
    return pl.BlockSpec(shape, lambda *_: (0,) * nd)


def _mixer(x, win, wa2, ba, gng, ws, bs_full, slg, slb, wout, lng, lnb):
    B, S, D = x.shape
    T = TOK_TILE
    n_tiles = B * S // T
    assert S % (2 * T) == 0
    x2d = x.reshape(B * S, D)
    consts = (win, wa2, ba, gng, ws, bs_full, slg, slb, wout, lng, lnb)
    pair = pl.BlockSpec((2 * T, D), lambda g: (g, 0))
    nxt = pl.BlockSpec((T, D), lambda g: (jnp.minimum(2 * g + 2, n_tiles - 1), 0))
    out = pl.pallas_call(
        functools.partial(_mixer_kernel, steps_per_seq=S // (2 * T)),
        grid=(n_tiles // 2,),
        in_specs=[pair, nxt] + [_const_spec(c.shape, single_buffer=True) for c in consts],
        out_specs=pair,
        out_shape=jax.ShapeDtypeStruct((B * S, D), F32),
        scratch_shapes=[
            pltpu.VMEM((D_GLA, D_QK), F32),
            pltpu.VMEM((T // CHUNK, 1, D_QK), F32),
            pltpu.VMEM((T // CHUNK, D_GLA, D_QK), F32),
            pltpu.VMEM((T // CHUNK, D_GLA, D_QK), BF16),
            pltpu.VMEM((T, D_INP), F32),
            pltpu.VMEM((T, D_INP), F32),
            pltpu.VMEM((T, D_QK), F32),
            pltpu.VMEM((T, D_QK), F32),
            pltpu.VMEM((T, D), BF16),
            pltpu.VMEM((T, D), BF16),
        ],
        compiler_params=pltpu.CompilerParams(
            dimension_semantics=("arbitrary",), vmem_limit_bytes=VMEM_LIMIT),
        name="mixer",
    )(x2d, x2d, *consts)
    return out.reshape(B, S, D)


def _route(logits):
    m = functools.reduce(jnp.maximum, logits)
    e = [jnp.exp(l - m) for l in logits]
    inv = 1.0 / functools.reduce(jnp.add, e)
    sc = [v * inv for v in e]

    def top2(a):
        first = functools.reduce(jnp.maximum, a)
        second = None
        for i in range(len(a)):
            for j in range(i + 1, len(a)):
                mn = jnp.minimum(a[i], a[j])
                second = mn if second is None else jnp.maximum(second, mn)
        return first + second

    gs = [top2(sc[g * EXPERTS_PER_GROUP:(g + 1) * EXPERTS_PER_GROUP]) for g in range(N_EXPERT_GROUPS)]
    best, g_sel = gs[0], jnp.zeros_like(gs[0])
    for g in range(1, N_EXPERT_GROUPS):
        better = gs[g] > best
        g_sel = jnp.where(better, float(g), g_sel)
        best = jnp.where(better, gs[g], best)
    a = []
    for j in range(EXPERTS_PER_GROUP):
        v = jnp.zeros_like(best)
        for g in range(N_EXPERT_GROUPS):
            v = v + jnp.where(g_sel == float(g), sc[g * EXPERTS_PER_GROUP + j], 0.0)
        a.append(v)
    w1, i1 = a[0], jnp.zeros_like(a[0])
    for j in range(1, EXPERTS_PER_GROUP):
        better = a[j] > w1
        i1 = jnp.where(better, float(j), i1)
        w1 = jnp.where(better, a[j], w1)
    w2, i2 = None, None
    for j in range(EXPERTS_PER_GROUP):
        cand = jnp.where(i1 == float(j), -1.0, a[j])
        if w2 is None:
            w2, i2 = cand, jnp.zeros_like(cand)
        else:
            better = cand > w2
            i2 = jnp.where(better, float(j), i2)
            w2 = jnp.where(better, cand, w2)
    tot = w1 + w2
    w1n, w2n = w1 / tot, w2 / tot
    first_is_lo = i1 < i2
    lo = jnp.where(first_is_lo, i1, i2)
    hi = jnp.where(first_is_lo, i2, i1)
    pair = jnp.zeros_like(lo)
    for pi, (pa, pb) in enumerate(PAIRS):
        pair = jnp.where((lo == float(pa)) & (hi == float(pb)), float(pi), pair)
    bucket = g_sel * float(len(PAIRS)) + pair
    g_lo = jnp.where(first_is_lo, w1n, w2n)
    g_hi = jnp.where(first_is_lo, w2n, w1n)
    return bucket, g_lo, g_hi


def _xattn_kernel(x_ref, mem_ref, wq_ref, wk_ref, wv_ref, wo_ref, lng_ref, lnb_ref, wr_ref, br_ref,
                  o_ref, route_ref, k_ref, v_ref, a_ref):
    @pl.when(pl.program_id(1) == 0)
    def _():
        mb = mem_ref[0].astype(BF16)
        k_ref[...] = jnp.dot(mb, wk_ref[...], preferred_element_type=F32).astype(BF16)
        v_ref[...] = jnp.dot(mb, wv_ref[...], preferred_element_type=F32).astype(BF16)

    x = x_ref[0]
    q = jnp.dot(x.astype(BF16), wq_ref[...], preferred_element_type=F32).astype(BF16)
    for h in range(XATTN_HEADS):
        cols = slice(h * XATTN_DH, (h + 1) * XATTN_DH)
        s = lax.dot_general(q[:, cols], k_ref[:, cols], (((1,), (1,)), ((), ())),
                            preferred_element_type=F32) * (XATTN_DH ** -0.5)
        e = jnp.exp(s - jnp.max(s, axis=-1, keepdims=True))
        p = e / jnp.sum(e, axis=-1, keepdims=True)
        a_ref[:, cols] = jnp.dot(p.astype(BF16), v_ref[:, cols],
                                 preferred_element_type=F32).astype(BF16)
    hres = jnp.dot(a_ref[...], wo_ref[...], preferred_element_type=F32)
    x2 = _layer_norm(DN_ALPHA * x + hres, lng_ref[...], lnb_ref[...])
    o_ref[0, :, 0:D_MODEL] = x2

    lt = lax.dot_general(wr_ref[...], x2.astype(BF16), (((1,), (1,)), ((), ())),
                         preferred_element_type=F32) + br_ref[...]
    bucket, g_lo, g_hi = _route([lt[i:i + 1, :] for i in range(N_EXPERTS)])
    route_ref[0, 0:1, :] = bucket
    route_ref[0, 1:2, :] = g_lo
    route_ref[0, 2:3, :] = g_hi
    route_ref[0, 3:8, :] = jnp.zeros((5, bucket.shape[1]), F32)
    rt = jnp.concatenate([bucket, g_lo, g_hi, jnp.zeros((LANES - 3, bucket.shape[1]), F32)], axis=0)
    o_ref[0, :, D_MODEL:D_EXT] = rt.T


def _xattn(x, mem, wq, wk, wv, wo, lng, lnb, wr_t, br_col):
    B, S, D = x.shape
    T = TOK_TILE
    nt = S // T
    tile = pl.BlockSpec((1, T, D), lambda b, s: (b, s, 0))
    consts = (wq, wk, wv, wo, lng, lnb, wr_t, br_col)
    return pl.pallas_call(
        _xattn_kernel,
        grid=(B, nt),
        in_specs=[tile, pl.BlockSpec((1, N_MEM, D), lambda b, s: (b, 0, 0))]
        + [_const_spec(c.shape) for c in consts],
        out_specs=[pl.BlockSpec((1, T, D_EXT), lambda b, s: (b, s, 0)),
                   pl.BlockSpec((1, 8, T), lambda b, s: (b * nt + s, 0, 0))],
        out_shape=[jax.ShapeDtypeStruct((B, S, D_EXT), F32),
                   jax.ShapeDtypeStruct((B * nt, 8, T), F32)],
        scratch_shapes=[
            pltpu.VMEM((N_MEM, D), BF16),
            pltpu.VMEM((N_MEM, D), BF16),
            pltpu.VMEM((T, D), BF16),
        ],
        compiler_params=pltpu.CompilerParams(
            dimension_semantics=("arbitrary", "arbitrary"), vmem_limit_bytes=VMEM_LIMIT),
        name="xattn",
    )(x, mem, *consts)


def _moe_kernel(elo_ref, ehi_ref, nrows_ref, start_ref, order_ref, x_hbm,
                wg_lo, wu_lo, wd_lo, wg_hi, wu_hi, wd_hi, lng_ref, lnb_ref, o_hbm,
                xbuf, obuf, gsem, ssem):
    del elo_ref, ehi_ref
    i = pl.program_id(0)
    n_tiles = pl.num_programs(0)
    tm = xbuf.shape[0] * SUBLANES
    slot = lax.rem(i, 2)
    nrows = nrows_ref[i]
    nxt = jnp.minimum(i + 1, n_tiles - 1)
    has_next = jnp.logical_and(i + 1 < n_tiles, nrows_ref[nxt] > 0)

    def gather_row(base, j, k):
        t = order_ref[base + j * SUBLANES + k]
        pltpu.make_async_copy(x_hbm.at[pl.ds(t, 1)], xbuf.at[j, pl.ds(k, 1)], gsem.at[0]).start(k % 2)

    def scatter_row(base, s, j, k):
        t = order_ref[base + j * SUBLANES + k]
        pltpu.make_async_copy(obuf.at[s, j, pl.ds(k, 1)], o_hbm.at[pl.ds(t, 1)], ssem.at[s]).start(k % 2)

    def wait_gather():
        pltpu.make_async_copy(xbuf, xbuf, gsem.at[0]).wait()

    def wait_scatter(s, n):
        @pl.when(n == tm)
        def _():
            pltpu.make_async_copy(obuf.at[s], obuf.at[s], ssem.at[s]).wait()

        @pl.when(n < tm)
        def _():
            def body(r, c):
                pltpu.make_async_copy(obuf.at[s, 0, pl.ds(0, 1)], obuf.at[s, 0, pl.ds(0, 1)],
                                      ssem.at[s]).wait()
                return c
            lax.fori_loop(0, n, body, 0)

    @pl.when(jnp.logical_and(i == 0, nrows > 0))
    def _():
        base0 = start_ref[0]

        def body(j, c):
            for k in range(SUBLANES):
                gather_row(base0, j, k)
            return c
        lax.fori_loop(0, tm // SUBLANES, body, 0)

    @pl.when(nrows > 0)
    def _():
        wait_gather()
        xe = xbuf[...].reshape(tm, xbuf.shape[2])
        x = xe[:, 0:D_MODEL]
        g_lo = xe[:, COL_GLO:COL_GLO + 1]
        g_hi = xe[:, COL_GHI:COL_GHI + 1]
        xb = x.astype(BF16)

        def expert(wg, wu, wd):
            hg = jnp.dot(xb, wg[0], preferred_element_type=F32)
            hu = jnp.dot(xb, wu[0], preferred_element_type=F32)
            return jnp.dot((_silu(hg) * hu).astype(BF16), wd[0], preferred_element_type=F32)

        y = g_lo * expert(wg_lo, wu_lo, wd_lo)
        y = y + g_hi * expert(wg_hi, wu_hi, wd_hi)
        res = _layer_norm(DN_ALPHA * x + y, lng_ref[...], lnb_ref[...])
        base_n = start_ref[nxt]
        for r in range(tm):
            gather_row(base_n, r // SUBLANES, r % SUBLANES)
        obuf[slot] = res.reshape(tm // SUBLANES, SUBLANES, D_MODEL)

        base = start_ref[i]
        for s in range(2):
            @pl.when(jnp.logical_and(nrows == tm, slot == s))
            def _(s=s):
                for r in range(tm):
                    scatter_row(base, s, r // SUBLANES, r % SUBLANES)

        @pl.when(nrows < tm)
        def _():
            def body(r, c):
                t = order_ref[base + r]
                pltpu.make_async_copy(obuf.at[slot, r // SUBLANES, pl.ds(r % SUBLANES, 1)],
                                      o_hbm.at[pl.ds(t, 1)], ssem.at[slot]).start()
                return c
            lax.fori_loop(0, nrows, body, 0)

        @pl.when(i > 0)
        def _():
            wait_scatter(1 - slot, nrows_ref[jnp.maximum(i - 1, 0)])

        @pl.when(jnp.logical_not(has_next))
        def _():
            wait_scatter(slot, nrows)
            wait_gather()


def _moe(x2e, order_pad, wg, wu, wd, lng, lnb, tile_elo, tile_ehi, tile_nrows, tile_start):
    n_tok, de = x2e.shape
    n_tiles = tile_start.shape[0]
    tm, D = MOE_TILE, D_MODEL
    wlo = lambda s: pl.BlockSpec((1,) + s, lambda i, elo, ehi, nr, st, od: (elo[i], 0, 0))
    whi = lambda s: pl.BlockSpec((1,) + s, lambda i, elo, ehi, nr, st, od: (ehi[i], 0, 0))
    cst = lambda s: pl.BlockSpec(s, lambda i, elo, ehi, nr, st, od: (0, 0))
    gu, dn = (D, D_EXPERT), (D_EXPERT, D)
    grid_spec = pltpu.PrefetchScalarGridSpec(
        num_scalar_prefetch=5,
        grid=(n_tiles,),
        in_specs=[pl.BlockSpec(memory_space=pl.ANY),
                  wlo(gu), wlo(gu), wlo(dn), whi(gu), whi(gu), whi(dn),
                  cst(lng.shape), cst(lnb.shape)],
        out_specs=pl.BlockSpec(memory_space=pl.ANY),
        scratch_shapes=[
            pltpu.VMEM((tm // SUBLANES, SUBLANES, de), F32),
            pltpu.VMEM((2, tm // SUBLANES, SUBLANES, D), F32),
            pltpu.SemaphoreType.DMA((1,)),
            pltpu.SemaphoreType.DMA((2,)),
        ],
    )
    return pl.pallas_call(
        _moe_kernel,
        grid_spec=grid_spec,
        out_shape=jax.ShapeDtypeStruct((n_tok, D), F32),
        compiler_params=pltpu.CompilerParams(
            dimension_semantics=("arbitrary",), vmem_limit_bytes=VMEM_LIMIT),
        name="moe",
    )(tile_elo, tile_ehi, tile_nrows, tile_start, order_pad, x2e, wg, wu, wd, wg, wu, wd, lng, lnb)


def _dispatch_plan(bucket, n_tok):
    tm = MOE_TILE
    n_tiles = n_tok // tm + N_BUCKETS
    order = jnp.argsort(bucket, stable=True).astype(jnp.int32)
    kb = jnp.arange(N_BUCKETS, dtype=jnp.int32)
    counts = jnp.sum((bucket[:, None] == kb[None, :]).astype(jnp.int32), axis=0)
    tiles_per = (counts + tm - 1) // tm
    tile_end = jnp.cumsum(tiles_per)
    tile_start = tile_end - tiles_per
    sorted_start = jnp.cumsum(counts) - counts
    n_valid = tile_end[-1]
    ti = jnp.arange(n_tiles, dtype=jnp.int32)
    valid = ti < n_valid
    tile_bucket = jnp.sum((ti[:, None] >= tile_end[None, :]).astype(jnp.int32), axis=1)
    last_bucket = jnp.sum((n_valid - 1 >= tile_end).astype(jnp.int32))
    tile_bucket = jnp.minimum(jnp.where(valid, tile_bucket, last_bucket), N_BUCKETS - 1)
    pa = jnp.asarray([p[0] for p in PAIRS], jnp.int32)
    pb = jnp.asarray([p[1] for p in PAIRS], jnp.int32)
    grp, pair = tile_bucket // len(PAIRS), tile_bucket % len(PAIRS)
    tile_elo = grp * EXPERTS_PER_GROUP + pa[pair]
    tile_ehi = grp * EXPERTS_PER_GROUP + pb[pair]
    first_row = (ti - tile_start[tile_bucket]) * tm
    tile_nrows = jnp.where(valid, jnp.clip(counts[tile_bucket] - first_row, 0, tm), 0)
    start = jnp.clip(sorted_start[tile_bucket] + first_row, 0, n_tok)
    order_pad = jnp.concatenate([order, jnp.zeros((tm,), jnp.int32)])
    return (order_pad, tile_elo.astype(jnp.int32), tile_ehi.astype(jnp.int32),
            tile_nrows.astype(jnp.int32), start.astype(jnp.int32))


def kernel(x, mem, w_in, w_a2, b_a, gla_norm_g, w_s, b_s, sgu_ln_g, sgu_ln_b, w_out, wq_x, wk_x, wv_x,
           wo_x, w_router, b_router, w_gate, w_up, w_down, ln_g, ln_b):
    B, S, D = x.shape
    n_tok = B * S

    segs = (slice(0, 256), slice(256, 512), slice(512, 1024), slice(1024, 1536),
            slice(1552, 2064), slice(2064, 2576))
    w_in_r = jnp.concatenate(
        [w_in[:, :, s] for s in segs]
        + [jnp.pad(w_in[:, :, 1536:1552], ((0, 0), (0, 0), (0, A_PAD - GATE_RANK)))], axis=-1).astype(BF16)
    w_a2_p = jnp.pad(w_a2, ((0, 0), (0, A_PAD - GATE_RANK), (0, 0))).astype(BF16)
    bs_full = jnp.broadcast_to(b_s[:, :, :, None], b_s.shape + (SGU_CH,))
    w_out_b, wq_b, wk_b, wv_b, wo_b = (w.astype(BF16) for w in (w_out, wq_x, wk_x, wv_x, wo_x))
    wg_b, wu_b, wd_b = (w.astype(BF16) for w in (w_gate, w_up, w_down))
    wr_t = w_router.T.astype(BF16)
    br_col = jnp.broadcast_to(b_router[:, None], (N_EXPERTS, TOK_TILE))
    row = lambda v: v.reshape(1, -1)

    for l in range(DEPTH):
        x = _mixer(x, w_in_r[l], w_a2_p[l], row(b_a[l]), row(gla_norm_g[l]), w_s[l], bs_full[l],
                   row(sgu_ln_g[l]), row(sgu_ln_b[l]), w_out_b[l], row(ln_g[l, 0]), row(ln_b[l, 0]))
        x2e, route = _xattn(x, mem, wq_b[l], wk_b[l], wv_b[l], wo_b[l], row(ln_g[l, 1]), row(ln_b[l, 1]),
                            wr_t, br_col)
        bucket = route[:, 0, :].reshape(-1).astype(jnp.int32)
        order_pad, t_elo, t_ehi, t_nrows, t_start = _dispatch_plan(bucket, n_tok)
        x = _moe(x2e.reshape(n_tok, D_EXT), order_pad, wg_b[l], wu_b[l], wd_b[l],
                 row(ln_g[l, 2]), row(ln_b[l, 2]), t_elo, t_ehi, t_nrows, t_start).reshape(B, S, D)
    return x
```

```python
import functools

import jax
import jax.numpy as jnp
from jax import lax
from jax.experimental import pallas as pl
from jax.experimental.pallas import tpu as pltpu

F32 = jnp.float32
BF16 = jnp.bfloat16

D_MODEL = 1024
DEPTH = 2
CHUNK = 64
N_MEM = 256
D_GLA = 512
D_SGU = 512
GLA_HEADS = 4
D_QK = 256
GLA_DK = 64
GLA_DV = 128
GATE_RANK = 16
GATE_TEMP = 16.0
SGU_GROUPS = 4
SGU_BLOCK = 128
SGU_CH = 128
XATTN_HEADS = 4
XATTN_DH = 256
N_EXPERTS = 16
N_EXPERT_GROUPS = 4
EXPERTS_PER_GROUP = 4
D_EXPERT = 512
DN_ALPHA = (2.0 * DEPTH) ** 0.25
LN_EPS = 1e-5
RMS_EPS = 1e-6

LANES = 128
SUBLANES = 8
A_PAD = LANES
C_Q, C_K, C_V, C_R = 0, 256, 512, 1024
C_U, C_VS, C_A = 1536, 2048, 2560
D_INP = C_A + A_PAD
D_EXT = D_MODEL + LANES
COL_GLO, COL_GHI = D_MODEL + 1, D_MODEL + 2

PAIRS = ((0, 1), (0, 2), (0, 3), (1, 2), (1, 3), (2, 3))
N_BUCKETS = N_EXPERT_GROUPS * len(PAIRS)
CNT_ROWS = 32

TOK_TILE = 512
MOE_TILE = 512
VMEM_LIMIT = 56 * 1024 * 1024


def _layer_norm(x, g, b):
    mu = jnp.mean(x, axis=-1, keepdims=True)
    xc = x - mu
    var = jnp.mean(xc * xc, axis=-1, keepdims=True)
    return xc * lax.rsqrt(var + LN_EPS) * g + b


def _gelu_tanh(x):
    return 0.5 * x * (1.0 + jnp.tanh(0.7978845608028654 * (x + 0.044715 * (x * x * x))))


def _silu(x):
    return x / (1.0 + jnp.exp(-x))


def _mixer_kernel(x_ref, xn_ref, win_ref, wa2_ref, ba_ref, gng_ref, ws_ref, bs_ref, slg_ref, slb_ref,
                  wout_ref, lng_ref, lnb_ref, o_ref, st_ref, dec_ref, upd_ref, snap_ref, p0_ref, p1_ref,
                  g0_ref, g1_ref, y0_ref, y1_ref, *, steps_per_seq):
    T = xn_ref.shape[0]
    g = pl.program_id(0)
    consts = (wa2_ref, ba_ref, gng_ref, ws_ref, bs_ref, slg_ref, slb_ref, wout_ref, lng_ref, lnb_ref)
    gla_scratch = (st_ref, dec_ref, upd_ref, snap_ref)

    @pl.when(lax.rem(g, steps_per_seq) == 0)
    def _():
        st_ref[...] = jnp.zeros_like(st_ref)

    @pl.when(g == 0)
    def _():
        p0_ref[...] = jnp.dot(x_ref[pl.ds(0, T), :].astype(BF16), win_ref[...], preferred_element_type=F32)

    def in_proj(x):
        return jnp.dot(x.astype(BF16), win_ref[...], preferred_element_type=F32)

    lo, hi = pl.ds(0, T), pl.ds(T, T)
    p1_ref[...] = in_proj(x_ref[hi, :])
    o_ref[lo, :] = _mixer_gate_stage(x_ref.at[lo, :], p0_ref, *gla_scratch, g0_ref, y0_ref, *consts)
    p0_ref[...] = in_proj(xn_ref[...])
    o_ref[hi, :] = _mixer_gate_stage(x_ref.at[hi, :], p1_ref, *gla_scratch, g1_ref, y1_ref, *consts)


def _mixer_gate_stage(x_ref, p_ref, st_ref, dec_ref, upd_ref, snap_ref, g_ref, y_ref, wa2_ref, ba_ref, gng_ref,
                      ws_ref, bs_ref, slg_ref, slb_ref, wout_ref, lng_ref, lnb_ref):
    T = p_ref.shape[0]

    z = jnp.dot(p_ref[:, C_A:C_A + A_PAD].astype(BF16), wa2_ref[...],
                preferred_element_type=F32) + ba_ref[...]
    g_ref[...] = (jnp.minimum(z, 0.0) - jnp.log1p(jnp.exp(-jnp.abs(z)))) * (1.0 / GATE_TEMP)

    ri = lax.broadcasted_iota(jnp.int32, (CHUNK, 3 * CHUNK), 0)
    ci = lax.broadcasted_iota(jnp.int32, (CHUNK, 3 * CHUNK), 1) % CHUNK
    tri3 = (ci <= ri).astype(BF16)

    def cumsum_chunk(gc):
        hi = gc.astype(BF16)
        r1 = gc - hi.astype(F32)
        mid = r1.astype(BF16)
        lo = (r1 - mid.astype(F32)).astype(BF16)
        return jnp.dot(tri3, jnp.concatenate([hi, mid, lo], axis=0), preferred_element_type=F32)
    hr = lax.broadcasted_iota(jnp.int32, (D_GLA, D_QK), 0) // GLA_DV
    hc = lax.broadcasted_iota(jnp.int32, (D_GLA, D_QK), 1) // GLA_DK
    head_mask = (hr == hc).astype(F32)

    n_chunks = T // CHUNK
    for c in range(n_chunks):
        rows = pl.ds(c * CHUNK, CHUNK)
        bcum = cumsum_chunk(g_ref[rows, :])
        b_last = bcum[CHUNK - 1:CHUNK, :]
        dec_ref[c] = jnp.exp(b_last)
        k_dec = p_ref[rows, C_K:C_K + D_QK] * jnp.exp(b_last - bcum)
        v_c = p_ref[rows, C_V:C_V + D_GLA]
        upd = lax.dot_general(v_c.astype(BF16), k_dec.astype(BF16), (((0,), (0,)), ((), ())),
                              preferred_element_type=F32)
        upd_ref[c] = upd * head_mask
    for c in range(n_chunks):
        st = st_ref[...] * dec_ref[c] + upd_ref[c]
        st_ref[...] = st
        snap_ref[c] = st.astype(BF16)
    for c in range(n_chunks):
        rows = pl.ds(c * CHUNK, CHUNK)
        q_c = p_ref[rows, C_Q:C_Q + D_QK] * (GLA_DK ** -0.5)
        o_c = lax.dot_general(q_c.astype(BF16), snap_ref[c], (((1,), (1,)), ((), ())),
                              preferred_element_type=F32)
        r_c = p_ref[rows, C_R:C_R + D_GLA]
        for h in range(GLA_HEADS):
            cols = slice(h * GLA_DV, (h + 1) * GLA_DV)
            oh = o_c[:, cols]
            oh = oh * lax.rsqrt(jnp.mean(oh * oh, axis=-1, keepdims=True) + RMS_EPS)
            y_ref[rows, cols] = (oh * gng_ref[:, cols] * _silu(r_c[:, cols])).astype(BF16)

    ti = lax.broadcasted_iota(jnp.int32, (SGU_BLOCK, SGU_BLOCK), 0) // CHUNK
    si = lax.broadcasted_iota(jnp.int32, (SGU_BLOCK, SGU_BLOCK), 1) // CHUNK
    causal = si <= ti
    for gi in range(SGU_GROUPS):
        w_m = jnp.where(causal, ws_ref[gi], 0.0).astype(BF16)
        cu = slice(C_U + gi * SGU_CH, C_U + (gi + 1) * SGU_CH)
        cv = slice(C_VS + gi * SGU_CH, C_VS + (gi + 1) * SGU_CH)
        cg = slice(gi * SGU_CH, (gi + 1) * SGU_CH)
        n_blocks = T // SGU_BLOCK
        vn = [_layer_norm(_gelu_tanh(p_ref[pl.ds(n * SGU_BLOCK, SGU_BLOCK), cv]),
                          slg_ref[:, cg], slb_ref[:, cg]).astype(BF16) for n in range(n_blocks)]
        mixed = jnp.dot(w_m, jnp.concatenate(vn, axis=1), preferred_element_type=F32)
        for n in range(n_blocks):
            rows = pl.ds(n * SGU_BLOCK, SGU_BLOCK)
            u = _gelu_tanh(p_ref[rows, cu])
            m_n = mixed[:, n * SGU_CH:(n + 1) * SGU_CH] + bs_ref[gi]
            y_ref[rows, D_GLA + gi * SGU_CH:D_GLA + (gi + 1) * SGU_CH] = (u * m_n).astype(BF16)

    h = jnp.dot(y_ref[...], wout_ref[...], preferred_element_type=F32)
    return _layer_norm(DN_ALPHA * x_ref[...] + h, lng_ref[...], lnb_ref[...])


def _const_spec(shape, single_buffer=False):
    nd = len(shape)
    if single_buffer:
        return pl.BlockSpec(shape, lambda *_: (0,) * nd, pipeline_mode=pl.Buffered(1))
    return pl.BlockSpec(shape, lambda *_: (0,) * nd)


def _mixer(x, win, wa2, ba, gng, ws, bs_full, slg, slb, wout, lng, lnb):
    B, S, D = x.shape
    T = TOK_TILE
    n_tiles = B * S // T
    assert S % (2 * T) == 0
    x2d = x.reshape(B * S, D)
    consts = (win, wa2, ba, gng, ws, bs_full, slg, slb, wout, lng, lnb)
    pair = pl.BlockSpec((2 * T, D), lambda g: (g, 0))
    nxt = pl.BlockSpec((T, D), lambda g: (jnp.minimum(2 * g + 2, n_tiles - 1), 0))
    out = pl.pallas_call(
        functools.partial(_mixer_kernel, steps_per_seq=S // (2 * T)),
        grid=(n_tiles // 2,),
        in_specs=[pair, nxt] + [_const_spec(c.shape, single_buffer=True) for c in consts],
        out_specs=pair,
        out_shape=jax.ShapeDtypeStruct((B * S, D), F32),
        scratch_shapes=[
            pltpu.VMEM((D_GLA, D_QK), F32),
            pltpu.VMEM((T // CHUNK, 1, D_QK), F32),
            pltpu.VMEM((T // CHUNK, D_GLA, D_QK), F32),
            pltpu.VMEM((T // CHUNK, D_GLA, D_QK), BF16),
            pltpu.VMEM((T, D_INP), F32),
            pltpu.VMEM((T, D_INP), F32),
            pltpu.VMEM((T, D_QK), F32),
            pltpu.VMEM((T, D_QK), F32),
            pltpu.VMEM((T, D), BF16),
            pltpu.VMEM((T, D), BF16),
        ],
        compiler_params=pltpu.CompilerParams(
            dimension_semantics=("arbitrary",), vmem_limit_bytes=VMEM_LIMIT),
        name="mixer",
    )(x2d, x2d, *consts)
    return out.reshape(B, S, D)


def _route(logits):
    m = functools.reduce(jnp.maximum, logits)
    e = [jnp.exp(l - m) for l in logits]
    inv = 1.0 / functools.reduce(jnp.add, e)
    sc = [v * inv for v in e]

    def top2(a):
        first = functools.reduce(jnp.maximum, a)
        second = None
        for i in range(len(a)):
            for j in range(i + 1, len(a)):
                mn = jnp.minimum(a[i], a[j])
                second = mn if second is None else jnp.maximum(second, mn)
        return first + second

    gs = [top2(sc[g * EXPERTS_PER_GROUP:(g + 1) * EXPERTS_PER_GROUP]) for g in range(N_EXPERT_GROUPS)]
    best, g_sel = gs[0], jnp.zeros_like(gs[0])
    for g in range(1, N_EXPERT_GROUPS):
        better = gs[g] > best
        g_sel = jnp.where(better, float(g), g_sel)
        best = jnp.where(better, gs[g], best)
    a = []
    for j in range(EXPERTS_PER_GROUP):
        v = jnp.zeros_like(best)
        for g in range(N_EXPERT_GROUPS):
            v = v + jnp.where(g_sel == float(g), sc[g * EXPERTS_PER_GROUP + j], 0.0)
        a.append(v)
    w1, i1 = a[0], jnp.zeros_like(a[0])
    for j in range(1, EXPERTS_PER_GROUP):
        better = a[j] > w1
        i1 = jnp.where(better, float(j), i1)
        w1 = jnp.where(better, a[j], w1)
    w2, i2 = None, None
    for j in range(EXPERTS_PER_GROUP):
        cand = jnp.where(i1 == float(j), -1.0, a[j])
        if w2 is None:
            w2, i2 = cand, jnp.zeros_like(cand)
        else:
            better = cand > w2
            i2 = jnp.where(better, float(j), i2)
            w2 = jnp.where(better, cand, w2)
    tot = w1 + w2
    w1n, w2n = w1 / tot, w2 / tot
    first_is_lo = i1 < i2
    lo = jnp.where(first_is_lo, i1, i2)
    hi = jnp.where(first_is_lo, i2, i1)
    pair = jnp.zeros_like(lo)
    for pi, (pa, pb) in enumerate(PAIRS):
        pair = jnp.where((lo == float(pa)) & (hi == float(pb)), float(pi), pair)
    bucket = g_sel * float(len(PAIRS)) + pair
    g_lo = jnp.where(first_is_lo, w1n, w2n)
    g_hi = jnp.where(first_is_lo, w2n, w1n)
    return bucket, g_lo, g_hi


def _xattn_kernel(x_ref, mem_ref, wq_ref, wk_ref, wv_ref, wo_ref, lng_ref, lnb_ref, wr_ref, br_ref,
                  o_ref, route_ref, cnt_ref, k_ref, v_ref, q0_ref, q1_ref, a0_ref, a1_ref):
    T = q0_ref.shape[0]

    @pl.when(pl.program_id(1) == 0)
    def _():
        mb = mem_ref[0].astype(BF16)
        k_ref[...] = jnp.dot(mb, wk_ref[...], preferred_element_type=F32).astype(BF16)
        v_ref[...] = jnp.dot(mb, wv_ref[...], preferred_element_type=F32).astype(BF16)

    @pl.when(jnp.logical_and(pl.program_id(0) == 0, pl.program_id(1) == 0))
    def _():
        cnt_ref[...] = jnp.zeros_like(cnt_ref)

    bucket_ids = lax.broadcasted_iota(jnp.int32, (cnt_ref.shape[0], 1), 0).astype(F32)
    for half, (q_ref, a_ref) in enumerate(((q0_ref, a0_ref), (q1_ref, a1_ref))):
        rows = pl.ds(half * T, T)
        q = jnp.dot(x_ref[0, rows, :].astype(BF16), wq_ref[...], preferred_element_type=F32)
        q_ref[...] = (q * (XATTN_DH ** -0.5)).astype(BF16)
        for h in range(XATTN_HEADS):
            cols = slice(h * XATTN_DH, (h + 1) * XATTN_DH)
            s = lax.dot_general(q_ref[:, cols], k_ref[:, cols], (((1,), (1,)), ((), ())),
                                preferred_element_type=F32)
            e = jnp.exp(s - jnp.max(s, axis=-1, keepdims=True))
            p = e / jnp.sum(e, axis=-1, keepdims=True)
            a_ref[:, cols] = jnp.dot(p.astype(BF16), v_ref[:, cols],
                                     preferred_element_type=F32).astype(BF16)
        hres = jnp.dot(a_ref[...], wo_ref[...], preferred_element_type=F32)
        x2 = _layer_norm(DN_ALPHA * x_ref[0, rows, :] + hres, lng_ref[...], lnb_ref[...])
        o_ref[0, rows, 0:D_MODEL] = x2

        lt = lax.dot_general(wr_ref[...], x2.astype(BF16), (((1,), (1,)), ((), ())),
                             preferred_element_type=F32) + br_ref[...]
        bucket, g_lo, g_hi = _route([lt[i:i + 1, :] for i in range(N_EXPERTS)])
        route_ref[0, 0:1, rows] = bucket
        cnt_ref[...] += jnp.sum((bucket == bucket_ids).astype(F32), axis=1, keepdims=True)
        rt = jnp.concatenate([bucket, g_lo, g_hi, jnp.zeros((LANES - 3, T), F32)], axis=0)
        o_ref[0, rows, D_MODEL:D_EXT] = rt.T


def _xattn(x, mem, wq, wk, wv, wo, lng, lnb, wr_t, br_col):
    B, S, D = x.shape
    T = TOK_TILE
    nt = S // (2 * T)
    tile = pl.BlockSpec((1, 2 * T, D), lambda b, s: (b, s, 0))
    consts = (wq, wk, wv, wo, lng, lnb, wr_t, br_col)
    return pl.pallas_call(
        _xattn_kernel,
        grid=(B, nt),
        in_specs=[tile, pl.BlockSpec((1, N_MEM, D), lambda b, s: (b, 0, 0))]
        + [_const_spec(c.shape, single_buffer=True) for c in consts],
        out_specs=[pl.BlockSpec((1, 2 * T, D_EXT), lambda b, s: (b, s, 0)),
                   pl.BlockSpec((1, 1, 2 * T), lambda b, s: (b * nt + s, 0, 0)),
                   pl.BlockSpec((CNT_ROWS, LANES), lambda b, s: (0, 0))],
        out_shape=[jax.ShapeDtypeStruct((B, S, D_EXT), F32),
                   jax.ShapeDtypeStruct((B * nt, 1, 2 * T), F32),
                   jax.ShapeDtypeStruct((CNT_ROWS, LANES), F32)],
        scratch_shapes=[
            pltpu.VMEM((N_MEM, D), BF16),
            pltpu.VMEM((N_MEM, D), BF16),
            pltpu.VMEM((T, D), BF16),
            pltpu.VMEM((T, D), BF16),
            pltpu.VMEM((T, D), BF16),
            pltpu.VMEM((T, D), BF16),
        ],
        compiler_params=pltpu.CompilerParams(
            dimension_semantics=("arbitrary", "arbitrary"), vmem_limit_bytes=VMEM_LIMIT),
        name="xattn",
    )(x, mem, *consts)


def _moe_kernel(elo_ref, ehi_ref, nrows_ref, start_ref, order_ref, x_hbm,
                wg_lo, wu_lo, wd_lo, wg_hi, wu_hi, wd_hi, lng_ref, lnb_ref, o_hbm,
                xbuf, obuf, gsem, ssem):
    del elo_ref, ehi_ref
    i = pl.program_id(0)
    n_tiles = pl.num_programs(0)
    tm = xbuf.shape[0] * SUBLANES
    slot = lax.rem(i, 2)
    nrows = nrows_ref[i]
    nxt = jnp.minimum(i + 1, n_tiles - 1)
    has_next = jnp.logical_and(i + 1 < n_tiles, nrows_ref[nxt] > 0)

    def gather_row(base, j, k):
        t = order_ref[base + j * SUBLANES + k]
        pltpu.make_async_copy(x_hbm.at[pl.ds(t, 1)], xbuf.at[j, pl.ds(k, 1)], gsem.at[0]).start(k % 2)

    def scatter_row(base, s, j, k):
        t = order_ref[base + j * SUBLANES + k]
        pltpu.make_async_copy(obuf.at[s, j, pl.ds(k, 1)], o_hbm.at[pl.ds(t, 1)], ssem.at[s]).start(k % 2)

    def wait_gather():
        pltpu.make_async_copy(xbuf, xbuf, gsem.at[0]).wait()

    def wait_scatter(s, n):
        @pl.when(n == tm)
        def _():
            pltpu.make_async_copy(obuf.at[s], obuf.at[s], ssem.at[s]).wait()

        @pl.when(n < tm)
        def _():
            def body(r, c):
                pltpu.make_async_copy(obuf.at[s, 0, pl.ds(0, 1)], obuf.at[s, 0, pl.ds(0, 1)],
                                      ssem.at[s]).wait()
                return c
            lax.fori_loop(0, n, body, 0)

    @pl.when(jnp.logical_and(i == 0, nrows > 0))
    def _():
        base0 = start_ref[0]

        def body(j, c):
            for k in range(SUBLANES):
                gather_row(base0, j, k)
            return c
        lax.fori_loop(0, tm // SUBLANES, body, 0)

    @pl.when(nrows > 0)
    def _():
        wait_gather()
        xe = xbuf[...].reshape(tm, xbuf.shape[2])
        x = xe[:, 0:D_MODEL]
        g_lo = xe[:, COL_GLO:COL_GLO + 1]
        g_hi = xe[:, COL_GHI:COL_GHI + 1]
        xb = x.astype(BF16)

        def expert(wg, wu, wd):
            hg = jnp.dot(xb, wg[0], preferred_element_type=F32)
            hu = jnp.dot(xb, wu[0], preferred_element_type=F32)
            return jnp.dot((_silu(hg) * hu).astype(BF16), wd[0], preferred_element_type=F32)

        y = g_lo * expert(wg_lo, wu_lo, wd_lo)
        y = y + g_hi * expert(wg_hi, wu_hi, wd_hi)
        res = _layer_norm(DN_ALPHA * x + y, lng_ref[...], lnb_ref[...])
        base_n = start_ref[nxt]
        for r in range(tm):
            gather_row(base_n, r // SUBLANES, r % SUBLANES)
        obuf[slot] = res.reshape(tm // SUBLANES, SUBLANES, D_MODEL)

        base = start_ref[i]
        for s in range(2):
            @pl.when(jnp.logical_and(nrows == tm, slot == s))
            def _(s=s):
                for r in range(tm):
                    scatter_row(base, s, r // SUBLANES, r % SUBLANES)

        @pl.when(nrows < tm)
        def _():
            def body(r, c):
                t = order_ref[base + r]
                pltpu.make_async_copy(obuf.at[slot, r // SUBLANES, pl.ds(r % SUBLANES, 1)],
                                      o_hbm.at[pl.ds(t, 1)], ssem.at[slot]).start()
                return c
            lax.fori_loop(0, nrows, body, 0)

        @pl.when(i > 0)
        def _():
            wait_scatter(1 - slot, nrows_ref[jnp.maximum(i - 1, 0)])

        @pl.when(jnp.logical_not(has_next))
        def _():
            wait_scatter(slot, nrows)
            wait_gather()


def _moe(x2e, order_pad, wg, wu, wd, lng, lnb, tile_elo, tile_ehi, tile_nrows, tile_start):
    n_tok, de = x2e.shape
    n_tiles = tile_start.shape[0]
    tm, D = MOE_TILE, D_MODEL
    wlo = lambda s: pl.BlockSpec((1,) + s, lambda i, elo, ehi, nr, st, od: (elo[i], 0, 0))
    whi = lambda s: pl.BlockSpec((1,) + s, lambda i, elo, ehi, nr, st, od: (ehi[i], 0, 0))
    cst = lambda s: pl.BlockSpec(s, lambda i, elo, ehi, nr, st, od: (0, 0))
    gu, dn = (D, D_EXPERT), (D_EXPERT, D)
    grid_spec = pltpu.PrefetchScalarGridSpec(
        num_scalar_prefetch=5,
        grid=(n_tiles,),
        in_specs=[pl.BlockSpec(memory_space=pl.ANY),
                  wlo(gu), wlo(gu), wlo(dn), whi(gu), whi(gu), whi(dn),
                  cst(lng.shape), cst(lnb.shape)],
        out_specs=pl.BlockSpec(memory_space=pl.ANY),
        scratch_shapes=[
            pltpu.VMEM((tm // SUBLANES, SUBLANES, de), F32),
            pltpu.VMEM((2, tm // SUBLANES, SUBLANES, D), F32),
            pltpu.SemaphoreType.DMA((1,)),
            pltpu.SemaphoreType.DMA((2,)),
        ],
    )
    return pl.pallas_call(
        _moe_kernel,
        grid_spec=grid_spec,
        out_shape=jax.ShapeDtypeStruct((n_tok, D), F32),
        compiler_params=pltpu.CompilerParams(
            dimension_semantics=("arbitrary",), vmem_limit_bytes=VMEM_LIMIT),
        name="moe",
    )(tile_elo, tile_ehi, tile_nrows, tile_start, order_pad, x2e, wg, wu, wd, wg, wu, wd, lng, lnb)


def _dispatch_plan(bucket, counts, n_tok):
    tm = MOE_TILE
    n_tiles = n_tok // tm + N_BUCKETS
    assert n_tok <= 1 << 16
    order = jnp.sort(bucket * (1 << 16) + jnp.arange(n_tok, dtype=jnp.int32)) & ((1 << 16) - 1)
    tiles_per = (counts + tm - 1) // tm
    tile_end = jnp.cumsum(tiles_per)
    tile_start = tile_end - tiles_per
    sorted_start = jnp.cumsum(counts) - counts
    n_valid = tile_end[-1]
    ti = jnp.arange(n_tiles, dtype=jnp.int32)
    valid = ti < n_valid
    tile_bucket = jnp.sum((ti[:, None] >= tile_end[None, :]).astype(jnp.int32), axis=1)
    last_bucket = jnp.sum((n_valid - 1 >= tile_end).astype(jnp.int32))
    tile_bucket = jnp.minimum(jnp.where(valid, tile_bucket, last_bucket), N_BUCKETS - 1)
    pa = jnp.asarray([p[0] for p in PAIRS], jnp.int32)
    pb = jnp.asarray([p[1] for p in PAIRS], jnp.int32)
    grp, pair = tile_bucket // len(PAIRS), tile_bucket % len(PAIRS)
    tile_elo = grp * EXPERTS_PER_GROUP + pa[pair]
    tile_ehi = grp * EXPERTS_PER_GROUP + pb[pair]
    first_row = (ti - tile_start[tile_bucket]) * tm
    tile_nrows = jnp.where(valid, jnp.clip(counts[tile_bucket] - first_row, 0, tm), 0)
    start = jnp.clip(sorted_start[tile_bucket] + first_row, 0, n_tok)
    order_pad = jnp.concatenate([order, jnp.zeros((tm,), jnp.int32)])
    return (order_pad, tile_elo.astype(jnp.int32), tile_ehi.astype(jnp.int32),
            tile_nrows.astype(jnp.int32), start.astype(jnp.int32))


def kernel(x, mem, w_in, w_a2, b_a, gla_norm_g, w_s, b_s, sgu_ln_g, sgu_ln_b, w_out, wq_x, wk_x, wv_x,
           wo_x, w_router, b_router, w_gate, w_up, w_down, ln_g, ln_b):
    B, S, D = x.shape
    n_tok = B * S

    segs = (slice(0, 256), slice(256, 512), slice(512, 1024), slice(1024, 1536),
            slice(1552, 2064), slice(2064, 2576))
    w_in_r = jnp.concatenate(
        [w_in[:, :, s] for s in segs]
        + [jnp.pad(w_in[:, :, 1536:1552], ((0, 0), (0, 0), (0, A_PAD - GATE_RANK)))], axis=-1).astype(BF16)
    w_a2_p = jnp.pad(w_a2, ((0, 0), (0, A_PAD - GATE_RANK), (0, 0))).astype(BF16)
    bs_full = jnp.broadcast_to(b_s[:, :, :, None], b_s.shape + (SGU_CH,))
    w_out_b, wq_b, wk_b, wv_b, wo_b = (w.astype(BF16) for w in (w_out, wq_x, wk_x, wv_x, wo_x))
    wg_b, wu_b, wd_b = (w.astype(BF16) for w in (w_gate, w_up, w_down))
    wr_t = w_router.T.astype(BF16)
    br_col = jnp.broadcast_to(b_router[:, None], (N_EXPERTS, TOK_TILE))
    row = lambda v: v.reshape(1, -1)

    for l in range(DEPTH):
        x = _mixer(x, w_in_r[l], w_a2_p[l], row(b_a[l]), row(gla_norm_g[l]), w_s[l], bs_full[l],
                   row(sgu_ln_g[l]), row(sgu_ln_b[l]), w_out_b[l], row(ln_g[l, 0]), row(ln_b[l, 0]))
        x2e, route, cnt = _xattn(x, mem, wq_b[l], wk_b[l], wv_b[l], wo_b[l], row(ln_g[l, 1]),
                                 row(ln_b[l, 1]), wr_t, br_col)
        bucket = route.reshape(-1).astype(jnp.int32)
        counts = cnt[:N_BUCKETS, 0].astype(jnp.int32)
        order_pad, t_elo, t_ehi, t_nrows, t_start = _dispatch_plan(bucket, counts, n_tok)
        x = _moe(x2e.reshape(n_tok, D_EXT), order_pad, wg_b[l], wu_b[l], wd_b[l],
                 row(ln_g[l, 2]), row(ln_b[l, 2]), t_elo, t_ehi, t_nrows, t_start).reshape(B, S, D)
    return x
```

```python
import functools

import jax
import jax.numpy as jnp
from jax import lax
from jax.experimental import pallas as pl
from jax.experimental.pallas import tpu as pltpu

F32 = jnp.float32
BF16 = jnp.bfloat16

D_MODEL = 1024
DEPTH = 2
CHUNK = 64
N_MEM = 256
D_GLA = 512
D_SGU = 512
GLA_HEADS = 4
D_QK = 256
GLA_DK = 64
GLA_DV = 128
GATE_RANK = 16
GATE_TEMP = 16.0
SGU_GROUPS = 4
SGU_BLOCK = 128
SGU_CH = 128
XATTN_HEADS = 4
XATTN_DH = 256
N_EXPERTS = 16
N_EXPERT_GROUPS = 4
EXPERTS_PER_GROUP = 4
D_EXPERT = 512
DN_ALPHA = (2.0 * DEPTH) ** 0.25
LN_EPS = 1e-5
RMS_EPS = 1e-6

LANES = 128
SUBLANES = 8
A_PAD = LANES
C_Q, C_K, C_V, C_R = 0, 256, 512, 1024
C_U, C_VS, C_A = 1536, 2048, 2560
D_INP = C_A + A_PAD
D_EXT = D_MODEL + LANES
COL_GLO, COL_GHI = D_MODEL + 1, D_MODEL + 2

PAIRS = ((0, 1), (0, 2), (0, 3), (1, 2), (1, 3), (2, 3))
N_BUCKETS = N_EXPERT_GROUPS * len(PAIRS)
CNT_ROWS = 32

TOK_TILE = 512
MOE_TILE = 512
VMEM_LIMIT = 56 * 1024 * 1024


def _layer_norm(x, g, b):
    mu = jnp.mean(x, axis=-1, keepdims=True)
    xc = x - mu
    var = jnp.mean(xc * xc, axis=-1, keepdims=True)
    return xc * lax.rsqrt(var + LN_EPS) * g + b


def _gelu_tanh(x):
    return 0.5 * x * (1.0 + jnp.tanh(0.7978845608028654 * (x + 0.044715 * (x * x * x))))


def _silu(x):
    return x / (1.0 + jnp.exp(-x))


def _mixer_kernel(x_ref, xn_ref, win_ref, wa2_ref, ba_ref, gng_ref, ws_ref, bs_ref, slg_ref, slb_ref,
                  wout_ref, lng_ref, lnb_ref, o_ref, st_ref, dec_ref, upd_ref, snap_ref, p0_ref, p1_ref,
                  g0_ref, g1_ref, y0_ref, y1_ref, *, steps_per_seq):
    T = xn_ref.shape[0]
    g = pl.program_id(0)
    consts = (wa2_ref, ba_ref, gng_ref, ws_ref, bs_ref, slg_ref, slb_ref, wout_ref, lng_ref, lnb_ref)
    gla_scratch = (st_ref, dec_ref, upd_ref, snap_ref)

    @pl.when(lax.rem(g, steps_per_seq) == 0)
    def _():
        st_ref[...] = jnp.zeros_like(st_ref)

    @pl.when(g == 0)
    def _():
        p0_ref[...] = jnp.dot(x_ref[pl.ds(0, T), :].astype(BF16), win_ref[...], preferred_element_type=F32)

    def in_proj(x):
        return jnp.dot(x.astype(BF16), win_ref[...], preferred_element_type=F32)

    lo, hi = pl.ds(0, T), pl.ds(T, T)
    p1_ref[...] = in_proj(x_ref[hi, :])
    o_ref[lo, :] = _mixer_gate_stage(x_ref.at[lo, :], p0_ref, *gla_scratch, g0_ref, y0_ref, *consts)
    p0_ref[...] = in_proj(xn_ref[...])
    o_ref[hi, :] = _mixer_gate_stage(x_ref.at[hi, :], p1_ref, *gla_scratch, g1_ref, y1_ref, *consts)


def _mixer_gate_stage(x_ref, p_ref, st_ref, dec_ref, upd_ref, snap_ref, g_ref, y_ref, wa2_ref, ba_ref, gng_ref,
                      ws_ref, bs_ref, slg_ref, slb_ref, wout_ref, lng_ref, lnb_ref):
    T = p_ref.shape[0]

    z = jnp.dot(p_ref[:, C_A:C_A + A_PAD].astype(BF16), wa2_ref[...],
                preferred_element_type=F32) + ba_ref[...]
    g_ref[...] = (jnp.minimum(z, 0.0) - jnp.log1p(jnp.exp(-jnp.abs(z)))) * (1.0 / GATE_TEMP)

    ri = lax.broadcasted_iota(jnp.int32, (CHUNK, 3 * CHUNK), 0)
    ci = lax.broadcasted_iota(jnp.int32, (CHUNK, 3 * CHUNK), 1) % CHUNK
    tri3 = (ci <= ri).astype(BF16)

    def cumsum_chunk(gc):
        hi = gc.astype(BF16)
        r1 = gc - hi.astype(F32)
        mid = r1.astype(BF16)
        lo = (r1 - mid.astype(F32)).astype(BF16)
        return jnp.dot(tri3, jnp.concatenate([hi, mid, lo], axis=0), preferred_element_type=F32)
    hr = lax.broadcasted_iota(jnp.int32, (D_GLA, D_QK), 0) // GLA_DV
    hc = lax.broadcasted_iota(jnp.int32, (D_GLA, D_QK), 1) // GLA_DK
    head_mask = (hr == hc).astype(F32)

    n_chunks = T // CHUNK
    for c in range(n_chunks):
        rows = pl.ds(c * CHUNK, CHUNK)
        bcum = cumsum_chunk(g_ref[rows, :])
        b_last = bcum[CHUNK - 1:CHUNK, :]
        dec_ref[c] = jnp.exp(b_last)
        k_dec = p_ref[rows, C_K:C_K + D_QK] * jnp.exp(b_last - bcum)
        v_c = p_ref[rows, C_V:C_V + D_GLA]
        upd = lax.dot_general(v_c.astype(BF16), k_dec.astype(BF16), (((0,), (0,)), ((), ())),
                              preferred_element_type=F32)
        upd_ref[c] = upd * head_mask
    for c in range(n_chunks):
        st = st_ref[...] * dec_ref[c] + upd_ref[c]
        st_ref[...] = st
        snap_ref[c] = st.astype(BF16)
    for c in range(n_chunks):
        rows = pl.ds(c * CHUNK, CHUNK)
        q_c = p_ref[rows, C_Q:C_Q + D_QK] * (GLA_DK ** -0.5)
        o_c = lax.dot_general(q_c.astype(BF16), snap_ref[c], (((1,), (1,)), ((), ())),
                              preferred_element_type=F32)
        r_c = p_ref[rows, C_R:C_R + D_GLA]
        for h in range(GLA_HEADS):
            cols = slice(h * GLA_DV, (h + 1) * GLA_DV)
            oh = o_c[:, cols]
            oh = oh * lax.rsqrt(jnp.mean(oh * oh, axis=-1, keepdims=True) + RMS_EPS)
            y_ref[rows, cols] = (oh * gng_ref[:, cols] * _silu(r_c[:, cols])).astype(BF16)

    ti = lax.broadcasted_iota(jnp.int32, (SGU_BLOCK, SGU_BLOCK), 0) // CHUNK
    si = lax.broadcasted_iota(jnp.int32, (SGU_BLOCK, SGU_BLOCK), 1) // CHUNK
    causal = si <= ti
    for gi in range(SGU_GROUPS):
        w_m = jnp.where(causal, ws_ref[gi], 0.0).astype(BF16)
        cu = slice(C_U + gi * SGU_CH, C_U + (gi + 1) * SGU_CH)
        cv = slice(C_VS + gi * SGU_CH, C_VS + (gi + 1) * SGU_CH)
        cg = slice(gi * SGU_CH, (gi + 1) * SGU_CH)
        n_blocks = T // SGU_BLOCK
        vn = [_layer_norm(_gelu_tanh(p_ref[pl.ds(n * SGU_BLOCK, SGU_BLOCK), cv]),
                          slg_ref[:, cg], slb_ref[:, cg]).astype(BF16) for n in range(n_blocks)]
        mixed = jnp.dot(w_m, jnp.concatenate(vn, axis=1), preferred_element_type=F32)
        for n in range(n_blocks):
            rows = pl.ds(n * SGU_BLOCK, SGU_BLOCK)
            u = _gelu_tanh(p_ref[rows, cu])
            m_n = mixed[:, n * SGU_CH:(n + 1) * SGU_CH] + bs_ref[gi]
            y_ref[rows, D_GLA + gi * SGU_CH:D_GLA + (gi + 1) * SGU_CH] = (u * m_n).astype(BF16)

    h = jnp.dot(y_ref[...], wout_ref[...], preferred_element_type=F32)
    return _layer_norm(DN_ALPHA * x_ref[...] + h, lng_ref[...], lnb_ref[...])


def _const_spec(shape, single_buffer=False):
    nd = len(shape)
    if single_buffer:
        return pl.BlockSpec(shape, lambda *_: (0,) * nd, pipeline_mode=pl.Buffered(1))
    return pl.BlockSpec(shape, lambda *_: (0,) * nd)


def _mixer(x, win, wa2, ba, gng, ws, bs_full, slg, slb, wout, lng, lnb):
    B, S, D = x.shape
    T = TOK_TILE
    n_tiles = B * S // T
    assert S % (2 * T) == 0
    x2d = x.reshape(B * S, D)
    consts = (win, wa2, ba, gng, ws, bs_full, slg, slb, wout, lng, lnb)
    pair = pl.BlockSpec((2 * T, D), lambda g: (g, 0))
    nxt = pl.BlockSpec((T, D), lambda g: (jnp.minimum(2 * g + 2, n_tiles - 1), 0))
    out = pl.pallas_call(
        functools.partial(_mixer_kernel, steps_per_seq=S // (2 * T)),
        grid=(n_tiles // 2,),
        in_specs=[pair, nxt] + [_const_spec(c.shape, single_buffer=True) for c in consts],
        out_specs=pair,
        out_shape=jax.ShapeDtypeStruct((B * S, D), F32),
        scratch_shapes=[
            pltpu.VMEM((D_GLA, D_QK), F32),
            pltpu.VMEM((T // CHUNK, 1, D_QK), F32),
            pltpu.VMEM((T // CHUNK, D_GLA, D_QK), F32),
            pltpu.VMEM((T // CHUNK, D_GLA, D_QK), BF16),
            pltpu.VMEM((T, D_INP), F32),
            pltpu.VMEM((T, D_INP), F32),
            pltpu.VMEM((T, D_QK), F32),
            pltpu.VMEM((T, D_QK), F32),
            pltpu.VMEM((T, D), BF16),
            pltpu.VMEM((T, D), BF16),
        ],
        compiler_params=pltpu.CompilerParams(
            dimension_semantics=("arbitrary",), vmem_limit_bytes=VMEM_LIMIT),
        name="mixer",
    )(x2d, x2d, *consts)
    return out.reshape(B, S, D)


def _route(logits):
    m = functools.reduce(jnp.maximum, logits)
    e = [jnp.exp(l - m) for l in logits]
    inv = 1.0 / functools.reduce(jnp.add, e)
    sc = [v * inv for v in e]

    def top2(a):
        first = functools.reduce(jnp.maximum, a)
        second = None
        for i in range(len(a)):
            for j in range(i + 1, len(a)):
                mn = jnp.minimum(a[i], a[j])
                second = mn if second is None else jnp.maximum(second, mn)
        return first + second

    gs = [top2(sc[g * EXPERTS_PER_GROUP:(g + 1) * EXPERTS_PER_GROUP]) for g in range(N_EXPERT_GROUPS)]
    best, g_sel = gs[0], jnp.zeros_like(gs[0])
    for g in range(1, N_EXPERT_GROUPS):
        better = gs[g] > best
        g_sel = jnp.where(better, float(g), g_sel)
        best = jnp.where(better, gs[g], best)
    a = []
    for j in range(EXPERTS_PER_GROUP):
        v = jnp.zeros_like(best)
        for g in range(N_EXPERT_GROUPS):
            v = v + jnp.where(g_sel == float(g), sc[g * EXPERTS_PER_GROUP + j], 0.0)
        a.append(v)
    w1, i1 = a[0], jnp.zeros_like(a[0])
    for j in range(1, EXPERTS_PER_GROUP):
        better = a[j] > w1
        i1 = jnp.where(better, float(j), i1)
        w1 = jnp.where(better, a[j], w1)
    w2, i2 = None, None
    for j in range(EXPERTS_PER_GROUP):
        cand = jnp.where(i1 == float(j), -1.0, a[j])
        if w2 is None:
            w2, i2 = cand, jnp.zeros_like(cand)
        else:
            better = cand > w2
            i2 = jnp.where(better, float(j), i2)
            w2 = jnp.where(better, cand, w2)
    tot = w1 + w2
    w1n, w2n = w1 / tot, w2 / tot
    first_is_lo = i1 < i2
    lo = jnp.where(first_is_lo, i1, i2)
    hi = jnp.where(first_is_lo, i2, i1)
    pair = jnp.zeros_like(lo)
    for pi, (pa, pb) in enumerate(PAIRS):
        pair = jnp.where((lo == float(pa)) & (hi == float(pb)), float(pi), pair)
    bucket = g_sel * float(len(PAIRS)) + pair
    g_lo = jnp.where(first_is_lo, w1n, w2n)
    g_hi = jnp.where(first_is_lo, w2n, w1n)
    return bucket, g_lo, g_hi


def _xattn_kernel(x_ref, mem_ref, wq32_ref, wk32_ref, wv32_ref, wo32_ref, lng_ref, lnb_ref, wr_ref, br_ref,
                  o_ref, route_ref, cnt_ref, wq_ref, wk_ref, wv_ref, wo_ref, k_ref, v_ref,
                  q0_ref, q1_ref, a0_ref, a1_ref):
    T = q0_ref.shape[0]

    @pl.when(jnp.logical_and(pl.program_id(0) == 0, pl.program_id(1) == 0))
    def _():
        cnt_ref[...] = jnp.zeros_like(cnt_ref)
        for w32_ref, w_ref in ((wq32_ref, wq_ref), (wk32_ref, wk_ref), (wv32_ref, wv_ref), (wo32_ref, wo_ref)):
            for r in range(0, w_ref.shape[0], LANES):
                w_ref[r:r + LANES, :] = w32_ref[r:r + LANES, :].astype(BF16)

    @pl.when(pl.program_id(1) == 0)
    def _():
        mb = mem_ref[0].astype(BF16)
        k_ref[...] = jnp.dot(mb, wk_ref[...], preferred_element_type=F32).astype(BF16)
        v_ref[...] = jnp.dot(mb, wv_ref[...], preferred_element_type=F32).astype(BF16)

    bucket_ids = lax.broadcasted_iota(jnp.int32, (cnt_ref.shape[0], 1), 0).astype(F32)
    for half, (q_ref, a_ref) in enumerate(((q0_ref, a0_ref), (q1_ref, a1_ref))):
        rows = pl.ds(half * T, T)
        q = jnp.dot(x_ref[0, rows, :].astype(BF16), wq_ref[...], preferred_element_type=F32)
        q_ref[...] = (q * (XATTN_DH ** -0.5)).astype(BF16)
        for h in range(XATTN_HEADS):
            cols = slice(h * XATTN_DH, (h + 1) * XATTN_DH)
            s = lax.dot_general(q_ref[:, cols], k_ref[:, cols], (((1,), (1,)), ((), ())),
                                preferred_element_type=F32)
            e = jnp.exp(s - jnp.max(s, axis=-1, keepdims=True))
            p = e / jnp.sum(e, axis=-1, keepdims=True)
            a_ref[:, cols] = jnp.dot(p.astype(BF16), v_ref[:, cols],
                                     preferred_element_type=F32).astype(BF16)
        hres = jnp.dot(a_ref[...], wo_ref[...], preferred_element_type=F32)
        x2 = _layer_norm(DN_ALPHA * x_ref[0, rows, :] + hres, lng_ref[...], lnb_ref[...])
        o_ref[0, rows, 0:D_MODEL] = x2

        lt = lax.dot_general(wr_ref[...], x2.astype(BF16), (((1,), (1,)), ((), ())),
                             preferred_element_type=F32) + br_ref[...]
        bucket, g_lo, g_hi = _route([lt[i:i + 1, :] for i in range(N_EXPERTS)])
        route_ref[0, 0:1, rows] = bucket
        cnt_ref[...] += jnp.sum((bucket == bucket_ids).astype(F32), axis=1, keepdims=True)
        rt = jnp.concatenate([bucket, g_lo, g_hi, jnp.zeros((LANES - 3, T), F32)], axis=0)
        o_ref[0, rows, D_MODEL:D_EXT] = rt.T


def _xattn(x, mem, layer, wq, wk, wv, wo, lng, lnb, wr_t, br_col):
    B, S, D = x.shape
    T = TOK_TILE
    nt = S // (2 * T)
    tile = pl.BlockSpec((1, 2 * T, D), lambda b, s: (b, s, 0))
    proj = pl.BlockSpec((None, D, D), lambda b, s: (layer, 0, 0), pipeline_mode=pl.Buffered(1))
    consts = (lng, lnb, wr_t, br_col)
    return pl.pallas_call(
        _xattn_kernel,
        grid=(B, nt),
        in_specs=[tile, pl.BlockSpec((1, N_MEM, D), lambda b, s: (b, 0, 0))] + [proj] * 4
        + [_const_spec(c.shape, single_buffer=True) for c in consts],
        out_specs=[pl.BlockSpec((1, 2 * T, D_EXT), lambda b, s: (b, s, 0)),
                   pl.BlockSpec((1, 1, 2 * T), lambda b, s: (b * nt + s, 0, 0)),
                   pl.BlockSpec((CNT_ROWS, LANES), lambda b, s: (0, 0))],
        out_shape=[jax.ShapeDtypeStruct((B, S, D_EXT), F32),
                   jax.ShapeDtypeStruct((B * nt, 1, 2 * T), F32),
                   jax.ShapeDtypeStruct((CNT_ROWS, LANES), F32)],
        scratch_shapes=[
            pltpu.VMEM((D, D), BF16),
            pltpu.VMEM((D, D), BF16),
            pltpu.VMEM((D, D), BF16),
            pltpu.VMEM((D, D), BF16),
            pltpu.VMEM((N_MEM, D), BF16),
            pltpu.VMEM((N_MEM, D), BF16),
            pltpu.VMEM((T, D), BF16),
            pltpu.VMEM((T, D), BF16),
            pltpu.VMEM((T, D), BF16),
            pltpu.VMEM((T, D), BF16),
        ],
        compiler_params=pltpu.CompilerParams(
            dimension_semantics=("arbitrary", "arbitrary"), vmem_limit_bytes=VMEM_LIMIT),
        name="xattn",
    )(x, mem, wq, wk, wv, wo, *consts)


def _moe_kernel(elo_ref, ehi_ref, nrows_ref, start_ref, order_ref, x_hbm,
                wg_lo, wu_lo, wd_lo, wg_hi, wu_hi, wd_hi, lng_ref, lnb_ref, o_hbm,
                xbuf, obuf, gsem, ssem):
    del elo_ref, ehi_ref
    i = pl.program_id(0)
    n_tiles = pl.num_programs(0)
    tm = xbuf.shape[0] * SUBLANES
    slot = lax.rem(i, 2)
    nrows = nrows_ref[i]
    nxt = jnp.minimum(i + 1, n_tiles - 1)
    has_next = jnp.logical_and(i + 1 < n_tiles, nrows_ref[nxt] > 0)

    def gather_row(base, j, k):
        t = order_ref[base + j * SUBLANES + k]
        pltpu.make_async_copy(x_hbm.at[pl.ds(t, 1)], xbuf.at[j, pl.ds(k, 1)], gsem.at[0]).start(k % 2)

    def scatter_row(base, s, j, k):
        t = order_ref[base + j * SUBLANES + k]
        pltpu.make_async_copy(obuf.at[s, j, pl.ds(k, 1)], o_hbm.at[pl.ds(t, 1)], ssem.at[s]).start(k % 2)

    def wait_gather():
        pltpu.make_async_copy(xbuf, xbuf, gsem.at[0]).wait()

    def wait_scatter(s, n):
        @pl.when(n == tm)
        def _():
            pltpu.make_async_copy(obuf.at[s], obuf.at[s], ssem.at[s]).wait()

        @pl.when(n < tm)
        def _():
            def body(r, c):
                pltpu.make_async_copy(obuf.at[s, 0, pl.ds(0, 1)], obuf.at[s, 0, pl.ds(0, 1)],
                                      ssem.at[s]).wait()
                return c
            lax.fori_loop(0, n, body, 0)

    @pl.when(jnp.logical_and(i == 0, nrows > 0))
    def _():
        base0 = start_ref[0]

        def body(j, c):
            for k in range(SUBLANES):
                gather_row(base0, j, k)
            return c
        lax.fori_loop(0, tm // SUBLANES, body, 0)

    @pl.when(nrows > 0)
    def _():
        wait_gather()
        xe = xbuf[...].reshape(tm, xbuf.shape[2])
        x = xe[:, 0:D_MODEL]
        g_lo = xe[:, COL_GLO:COL_GLO + 1]
        g_hi = xe[:, COL_GHI:COL_GHI + 1]
        xb = x.astype(BF16)

        def expert(wg, wu, wd):
            hg = jnp.dot(xb, wg[0], preferred_element_type=F32)
            hu = jnp.dot(xb, wu[0], preferred_element_type=F32)
            return jnp.dot((_silu(hg) * hu).astype(BF16), wd[0], preferred_element_type=F32)

        y = g_lo * expert(wg_lo, wu_lo, wd_lo)
        y = y + g_hi * expert(wg_hi, wu_hi, wd_hi)
        res = _layer_norm(DN_ALPHA * x + y, lng_ref[...], lnb_ref[...])
        base_n = start_ref[nxt]
        for r in range(tm):
            gather_row(base_n, r // SUBLANES, r % SUBLANES)
        obuf[slot] = res.reshape(tm // SUBLANES, SUBLANES, D_MODEL)

        base = start_ref[i]
        for s in range(2):
            @pl.when(jnp.logical_and(nrows == tm, slot == s))
            def _(s=s):
                for r in range(tm):
                    scatter_row(base, s, r // SUBLANES, r % SUBLANES)

        @pl.when(nrows < tm)
        def _():
            def body(r, c):
                t = order_ref[base + r]
                pltpu.make_async_copy(obuf.at[slot, r // SUBLANES, pl.ds(r % SUBLANES, 1)],
                                      o_hbm.at[pl.ds(t, 1)], ssem.at[slot]).start()
                return c
            lax.fori_loop(0, nrows, body, 0)

        @pl.when(i > 0)
        def _():
            wait_scatter(1 - slot, nrows_ref[jnp.maximum(i - 1, 0)])

        @pl.when(jnp.logical_not(has_next))
        def _():
            wait_scatter(slot, nrows)
            wait_gather()


def _moe(x2e, order_pad, wg, wu, wd, lng, lnb, tile_elo, tile_ehi, tile_nrows, tile_start):
    n_tok, de = x2e.shape
    n_tiles = tile_start.shape[0]
    tm, D = MOE_TILE, D_MODEL
    wlo = lambda s: pl.BlockSpec((1,) + s, lambda i, elo, ehi, nr, st, od: (elo[i], 0, 0))
    whi = lambda s: pl.BlockSpec((1,) + s, lambda i, elo, ehi, nr, st, od: (ehi[i], 0, 0))
    cst = lambda s: pl.BlockSpec(s, lambda i, elo, ehi, nr, st, od: (0, 0))
    gu, dn = (D, D_EXPERT), (D_EXPERT, D)
    grid_spec = pltpu.PrefetchScalarGridSpec(
        num_scalar_prefetch=5,
        grid=(n_tiles,),
        in_specs=[pl.BlockSpec(memory_space=pl.ANY),
                  wlo(gu), wlo(gu), wlo(dn), whi(gu), whi(gu), whi(dn),
                  cst(lng.shape), cst(lnb.shape)],
        out_specs=pl.BlockSpec(memory_space=pl.ANY),
        scratch_shapes=[
            pltpu.VMEM((tm // SUBLANES, SUBLANES, de), F32),
            pltpu.VMEM((2, tm // SUBLANES, SUBLANES, D), F32),
            pltpu.SemaphoreType.DMA((1,)),
            pltpu.SemaphoreType.DMA((2,)),
        ],
    )
    return pl.pallas_call(
        _moe_kernel,
        grid_spec=grid_spec,
        out_shape=jax.ShapeDtypeStruct((n_tok, D), F32),
        compiler_params=pltpu.CompilerParams(
            dimension_semantics=("arbitrary",), vmem_limit_bytes=VMEM_LIMIT),
        name="moe",
    )(tile_elo, tile_ehi, tile_nrows, tile_start, order_pad, x2e, wg, wu, wd, wg, wu, wd, lng, lnb)


def _dispatch_plan(bucket, counts, n_tok):
    tm = MOE_TILE
    n_tiles = n_tok // tm + N_BUCKETS
    assert n_tok <= 1 << 16
    order = jnp.sort(bucket * (1 << 16) + jnp.arange(n_tok, dtype=jnp.int32)) & ((1 << 16) - 1)
    tiles_per = (counts + tm - 1) // tm
    tile_end = jnp.cumsum(tiles_per)
    tile_start = tile_end - tiles_per
    sorted_start = jnp.cumsum(counts) - counts
    n_valid = tile_end[-1]
    ti = jnp.arange(n_tiles, dtype=jnp.int32)
    valid = ti < n_valid
    tile_bucket = jnp.sum((ti[:, None] >= tile_end[None, :]).astype(jnp.int32), axis=1)
    last_bucket = jnp.sum((n_valid - 1 >= tile_end).astype(jnp.int32))
    tile_bucket = jnp.minimum(jnp.where(valid, tile_bucket, last_bucket), N_BUCKETS - 1)
    pa = jnp.asarray([p[0] for p in PAIRS], jnp.int32)
    pb = jnp.asarray([p[1] for p in PAIRS], jnp.int32)
    grp, pair = tile_bucket // len(PAIRS), tile_bucket % len(PAIRS)
    tile_elo = grp * EXPERTS_PER_GROUP + pa[pair]
    tile_ehi = grp * EXPERTS_PER_GROUP + pb[pair]
    first_row = (ti - tile_start[tile_bucket]) * tm
    tile_nrows = jnp.where(valid, jnp.clip(counts[tile_bucket] - first_row, 0, tm), 0)
    start = jnp.clip(sorted_start[tile_bucket] + first_row, 0, n_tok)
    order_pad = jnp.concatenate([order, jnp.zeros((tm,), jnp.int32)])
    return (order_pad, tile_elo.astype(jnp.int32), tile_ehi.astype(jnp.int32),
            tile_nrows.astype(jnp.int32), start.astype(jnp.int32))


def kernel(x, mem, w_in, w_a2, b_a, gla_norm_g, w_s, b_s, sgu_ln_g, sgu_ln_b, w_out, wq_x, wk_x, wv_x,
           wo_x, w_router, b_router, w_gate, w_up, w_down, ln_g, ln_b):
    B, S, D = x.shape
    n_tok = B * S

    segs = (slice(0, 256), slice(256, 512), slice(512, 1024), slice(1024, 1536),
            slice(1552, 2064), slice(2064, 2576))
    w_in_r = jnp.concatenate(
        [w_in[:, :, s] for s in segs]
        + [jnp.pad(w_in[:, :, 1536:1552], ((0, 0), (0, 0), (0, A_PAD - GATE_RANK)))], axis=-1).astype(BF16)
    w_a2_p = jnp.pad(w_a2, ((0, 0), (0, A_PAD - GATE_RANK), (0, 0))).astype(BF16)
    bs_full = jnp.broadcast_to(b_s[:, :, :, None], b_s.shape + (SGU_CH,))
    w_out_b = w_out.astype(BF16)
    wg_b, wu_b, wd_b = (w.astype(BF16) for w in (w_gate, w_up, w_down))
    wr_t = w_router.T.astype(BF16)
    br_col = jnp.broadcast_to(b_router[:, None], (N_EXPERTS, TOK_TILE))
    row = lambda v: v.reshape(1, -1)

    for l in range(DEPTH):
        x = _mixer(x, w_in_r[l], w_a2_p[l], row(b_a[l]), row(gla_norm_g[l]), w_s[l], bs_full[l],
                   row(sgu_ln_g[l]), row(sgu_ln_b[l]), w_out_b[l], row(ln_g[l, 0]), row(ln_b[l, 0]))
        x2e, route, cnt = _xattn(x, mem, l, wq_x, wk_x, wv_x, wo_x, row(ln_g[l, 1]), row(ln_b[l, 1]),
                                 wr_t, br_col)
        bucket = route.reshape(-1).astype(jnp.int32)
        counts = cnt[:N_BUCKETS, 0].astype(jnp.int32)
        order_pad, t_elo, t_ehi, t_nrows, t_start = _dispatch_plan(bucket, counts, n_tok)
        x = _moe(x2e.reshape(n_tok, D_EXT), order_pad, wg_b[l], wu_b[l], wd_b[l],
                 row(ln_g[l, 2]), row(ln_b[l, 2]), t_elo, t_ehi, t_nrows, t_start).reshape(B, S, D)
    return x
```

```python
import functools

import jax
import jax.numpy as jnp
from jax import lax
from jax.experimental import pallas as pl
from jax.experimental.pallas import tpu as pltpu

F32 = jnp.float32
BF16 = jnp.bfloat16

D_MODEL = 1024
DEPTH = 2
CHUNK = 64
N_MEM = 256
D_GLA = 512
D_SGU = 512
GLA_HEADS = 4
D_QK = 256
GLA_DK = 64
GLA_DV = 128
GATE_RANK = 16
GATE_TEMP = 16.0
SGU_GROUPS = 4
SGU_BLOCK = 128
SGU_CH = 128
XATTN_HEADS = 4
XATTN_DH = 256
N_EXPERTS = 16
N_EXPERT_GROUPS = 4
EXPERTS_PER_GROUP = 4
D_EXPERT = 512
DN_ALPHA = (2.0 * DEPTH) ** 0.25
LN_EPS = 1e-5
RMS_EPS = 1e-6

LANES = 128
SUBLANES = 8
A_PAD = LANES
C_Q, C_K, C_V, C_R = 0, 256, 512, 1024
C_U, C_VS, C_A = 1536, 2048, 2560
D_INP = C_A + A_PAD
D_EXT = D_MODEL + LANES
COL_GLO, COL_GHI = D_MODEL + 1, D_MODEL + 2

PAIRS = ((0, 1), (0, 2), (0, 3), (1, 2), (1, 3), (2, 3))
N_BUCKETS = N_EXPERT_GROUPS * len(PAIRS)
CNT_ROWS = 32

TOK_TILE = 512
MOE_TILE = 512
VMEM_LIMIT = 56 * 1024 * 1024


def _layer_norm(x, g, b):
    mu = jnp.mean(x, axis=-1, keepdims=True)
    xc = x - mu
    var = jnp.mean(xc * xc, axis=-1, keepdims=True)
    return xc * lax.rsqrt(var + LN_EPS) * g + b


def _gelu_tanh(x):
    return 0.5 * x * (1.0 + jnp.tanh(0.7978845608028654 * (x + 0.044715 * (x * x * x))))


def _silu(x):
    return x / (1.0 + jnp.exp(-x))


def _mixer_kernel(x_ref, xn_ref, win_ref, wa2_ref, ba_ref, gng_ref, ws_ref, bs_ref, slg_ref, slb_ref,
                  wout_ref, lng_ref, lnb_ref, o_ref, st_ref, dec_ref, upd_ref, snap_ref, p0_ref, p1_ref,
                  g0_ref, g1_ref, y0_ref, y1_ref, *, steps_per_seq):
    T = xn_ref.shape[0]
    g = pl.program_id(0)
    consts = (wa2_ref, ba_ref, gng_ref, ws_ref, bs_ref, slg_ref, slb_ref, wout_ref, lng_ref, lnb_ref)
    gla_scratch = (st_ref, dec_ref, upd_ref, snap_ref)

    @pl.when(lax.rem(g, steps_per_seq) == 0)
    def _():
        st_ref[...] = jnp.zeros_like(st_ref)

    @pl.when(g == 0)
    def _():
        p0_ref[...] = jnp.dot(x_ref[pl.ds(0, T), :].astype(BF16), win_ref[...], preferred_element_type=F32)

    def in_proj(x):
        return jnp.dot(x.astype(BF16), win_ref[...], preferred_element_type=F32)

    lo, hi = pl.ds(0, T), pl.ds(T, T)
    p1_ref[...] = in_proj(x_ref[hi, :])
    o_ref[lo, :] = _mixer_gate_stage(x_ref.at[lo, :], p0_ref, *gla_scratch, g0_ref, y0_ref, *consts)
    p0_ref[...] = in_proj(xn_ref[...])
    o_ref[hi, :] = _mixer_gate_stage(x_ref.at[hi, :], p1_ref, *gla_scratch, g1_ref, y1_ref, *consts)


def _mixer_gate_stage(x_ref, p_ref, st_ref, dec_ref, upd_ref, snap_ref, g_ref, y_ref, wa2_ref, ba_ref, gng_ref,
                      ws_ref, bs_ref, slg_ref, slb_ref, wout_ref, lng_ref, lnb_ref):
    T = p_ref.shape[0]

    z = jnp.dot(p_ref[:, C_A:C_A + A_PAD].astype(BF16), wa2_ref[...],
                preferred_element_type=F32) + ba_ref[...]
    g_ref[...] = (jnp.minimum(z, 0.0) - jnp.log1p(jnp.exp(-jnp.abs(z)))) * (1.0 / GATE_TEMP)

    ri = lax.broadcasted_iota(jnp.int32, (CHUNK, 3 * CHUNK), 0)
    ci = lax.broadcasted_iota(jnp.int32, (CHUNK, 3 * CHUNK), 1) % CHUNK
    tri3 = (ci <= ri).astype(BF16)

    def cumsum_chunk(gc):
        hi = gc.astype(BF16)
        r1 = gc - hi.astype(F32)
        mid = r1.astype(BF16)
        lo = (r1 - mid.astype(F32)).astype(BF16)
        return jnp.dot(tri3, jnp.concatenate([hi, mid, lo], axis=0), preferred_element_type=F32)
    hr = lax.broadcasted_iota(jnp.int32, (D_GLA, D_QK), 0) // GLA_DV
    hc = lax.broadcasted_iota(jnp.int32, (D_GLA, D_QK), 1) // GLA_DK
    head_mask = (hr == hc).astype(F32)

    n_chunks = T // CHUNK
    for c in range(n_chunks):
        rows = pl.ds(c * CHUNK, CHUNK)
        bcum = cumsum_chunk(g_ref[rows, :])
        b_last = bcum[CHUNK - 1:CHUNK, :]
        dec_ref[c] = jnp.exp(b_last)
        k_dec = p_ref[rows, C_K:C_K + D_QK] * jnp.exp(b_last - bcum)
        v_c = p_ref[rows, C_V:C_V + D_GLA]
        upd = lax.dot_general(v_c.astype(BF16), k_dec.astype(BF16), (((0,), (0,)), ((), ())),
                              preferred_element_type=F32)
        upd_ref[c] = upd * head_mask
    for c in range(n_chunks):
        st = st_ref[...] * dec_ref[c] + upd_ref[c]
        st_ref[...] = st
        snap_ref[c] = st.astype(BF16)
    for c in range(n_chunks):
        rows = pl.ds(c * CHUNK, CHUNK)
        q_c = p_ref[rows, C_Q:C_Q + D_QK] * (GLA_DK ** -0.5)
        o_c = lax.dot_general(q_c.astype(BF16), snap_ref[c], (((1,), (1,)), ((), ())),
                              preferred_element_type=F32)
        r_c = p_ref[rows, C_R:C_R + D_GLA]
        for h in range(GLA_HEADS):
            cols = slice(h * GLA_DV, (h + 1) * GLA_DV)
            oh = o_c[:, cols]
            oh = oh * lax.rsqrt(jnp.mean(oh * oh, axis=-1, keepdims=True) + RMS_EPS)
            y_ref[rows, cols] = (oh * gng_ref[:, cols] * _silu(r_c[:, cols])).astype(BF16)

    ti = lax.broadcasted_iota(jnp.int32, (SGU_BLOCK, SGU_BLOCK), 0) // CHUNK
    si = lax.broadcasted_iota(jnp.int32, (SGU_BLOCK, SGU_BLOCK), 1) // CHUNK
    causal = si <= ti
    for gi in range(SGU_GROUPS):
        w_m = jnp.where(causal, ws_ref[gi], 0.0).astype(BF16)
        cu = slice(C_U + gi * SGU_CH, C_U + (gi + 1) * SGU_CH)
        cv = slice(C_VS + gi * SGU_CH, C_VS + (gi + 1) * SGU_CH)
        cg = slice(gi * SGU_CH, (gi + 1) * SGU_CH)
        n_blocks = T // SGU_BLOCK
        vn = [_layer_norm(_gelu_tanh(p_ref[pl.ds(n * SGU_BLOCK, SGU_BLOCK), cv]),
                          slg_ref[:, cg], slb_ref[:, cg]).astype(BF16) for n in range(n_blocks)]
        mixed = jnp.dot(w_m, jnp.concatenate(vn, axis=1), preferred_element_type=F32)
        for n in range(n_blocks):
            rows = pl.ds(n * SGU_BLOCK, SGU_BLOCK)
            u = _gelu_tanh(p_ref[rows, cu])
            m_n = mixed[:, n * SGU_CH:(n + 1) * SGU_CH] + bs_ref[gi]
            y_ref[rows, D_GLA + gi * SGU_CH:D_GLA + (gi + 1) * SGU_CH] = (u * m_n).astype(BF16)

    h = jnp.dot(y_ref[...], wout_ref[...], preferred_element_type=F32)
    return _layer_norm(DN_ALPHA * x_ref[...] + h, lng_ref[...], lnb_ref[...])


def _const_spec(shape, single_buffer=False):
    nd = len(shape)
    if single_buffer:
        return pl.BlockSpec(shape, lambda *_: (0,) * nd, pipeline_mode=pl.Buffered(1))
    return pl.BlockSpec(shape, lambda *_: (0,) * nd)


def _mixer(x, win, wa2, ba, gng, ws, bs_full, slg, slb, wout, lng, lnb):
    B, S, D = x.shape
    T = TOK_TILE
    n_tiles = B * S // T
    assert S % (2 * T) == 0
    x2d = x.reshape(B * S, D)
    consts = (win, wa2, ba, gng, ws, bs_full, slg, slb, wout, lng, lnb)
    pair = pl.BlockSpec((2 * T, D), lambda g: (g, 0))
    nxt = pl.BlockSpec((T, D), lambda g: (jnp.minimum(2 * g + 2, n_tiles - 1), 0))
    out = pl.pallas_call(
        functools.partial(_mixer_kernel, steps_per_seq=S // (2 * T)),
        grid=(n_tiles // 2,),
        in_specs=[pair, nxt] + [_const_spec(c.shape, single_buffer=True) for c in consts],
        out_specs=pair,
        out_shape=jax.ShapeDtypeStruct((B * S, D), F32),
        scratch_shapes=[
            pltpu.VMEM((D_GLA, D_QK), F32),
            pltpu.VMEM((T // CHUNK, 1, D_QK), F32),
            pltpu.VMEM((T // CHUNK, D_GLA, D_QK), F32),
            pltpu.VMEM((T // CHUNK, D_GLA, D_QK), BF16),
            pltpu.VMEM((T, D_INP), F32),
            pltpu.VMEM((T, D_INP), F32),
            pltpu.VMEM((T, D_QK), F32),
            pltpu.VMEM((T, D_QK), F32),
            pltpu.VMEM((T, D), BF16),
            pltpu.VMEM((T, D), BF16),
        ],
        compiler_params=pltpu.CompilerParams(
            dimension_semantics=("arbitrary",), vmem_limit_bytes=VMEM_LIMIT),
        name="mixer",
    )(x2d, x2d, *consts)
    return out.reshape(B, S, D)


def _route(logits):
    m = functools.reduce(jnp.maximum, logits)
    e = [jnp.exp(l - m) for l in logits]
    inv = 1.0 / functools.reduce(jnp.add, e)
    sc = [v * inv for v in e]

    def top2(a):
        first = functools.reduce(jnp.maximum, a)
        second = None
        for i in range(len(a)):
            for j in range(i + 1, len(a)):
                mn = jnp.minimum(a[i], a[j])
                second = mn if second is None else jnp.maximum(second, mn)
        return first + second

    gs = [top2(sc[g * EXPERTS_PER_GROUP:(g + 1) * EXPERTS_PER_GROUP]) for g in range(N_EXPERT_GROUPS)]
    best, g_sel = gs[0], jnp.zeros_like(gs[0])
    for g in range(1, N_EXPERT_GROUPS):
        better = gs[g] > best
        g_sel = jnp.where(better, float(g), g_sel)
        best = jnp.where(better, gs[g], best)
    a = []
    for j in range(EXPERTS_PER_GROUP):
        v = jnp.zeros_like(best)
        for g in range(N_EXPERT_GROUPS):
            v = v + jnp.where(g_sel == float(g), sc[g * EXPERTS_PER_GROUP + j], 0.0)
        a.append(v)
    w1, i1 = a[0], jnp.zeros_like(a[0])
    for j in range(1, EXPERTS_PER_GROUP):
        better = a[j] > w1
        i1 = jnp.where(better, float(j), i1)
        w1 = jnp.where(better, a[j], w1)
    w2, i2 = None, None
    for j in range(EXPERTS_PER_GROUP):
        cand = jnp.where(i1 == float(j), -1.0, a[j])
        if w2 is None:
            w2, i2 = cand, jnp.zeros_like(cand)
        else:
            better = cand > w2
            i2 = jnp.where(better, float(j), i2)
            w2 = jnp.where(better, cand, w2)
    tot = w1 + w2
    w1n, w2n = w1 / tot, w2 / tot
    first_is_lo = i1 < i2
    lo = jnp.where(first_is_lo, i1, i2)
    hi = jnp.where(first_is_lo, i2, i1)
    pair = jnp.zeros_like(lo)
    for pi, (pa, pb) in enumerate(PAIRS):
        pair = jnp.where((lo == float(pa)) & (hi == float(pb)), float(pi), pair)
    bucket = g_sel * float(len(PAIRS)) + pair
    g_lo = jnp.where(first_is_lo, w1n, w2n)
    g_hi = jnp.where(first_is_lo, w2n, w1n)
    return bucket, g_lo, g_hi


def _xattn_kernel(x_ref, mem_ref, wq32_ref, wk32_ref, wv32_ref, wo32_ref, lng_ref, lnb_ref, wr_ref, br_ref,
                  o_ref, route_ref, cnt_ref, wq_ref, wk_ref, wv_ref, wo_ref, k_ref, v_ref,
                  q0_ref, q1_ref, a0_ref, a1_ref):
    T = q0_ref.shape[0]

    @pl.when(jnp.logical_and(pl.program_id(0) == 0, pl.program_id(1) == 0))
    def _():
        cnt_ref[...] = jnp.zeros_like(cnt_ref)
        for w32_ref, w_ref in ((wq32_ref, wq_ref), (wk32_ref, wk_ref), (wv32_ref, wv_ref), (wo32_ref, wo_ref)):
            for r in range(0, w_ref.shape[0], LANES):
                w_ref[r:r + LANES, :] = w32_ref[r:r + LANES, :].astype(BF16)

    @pl.when(pl.program_id(1) == 0)
    def _():
        mb = mem_ref[0].astype(BF16)
        k_ref[...] = jnp.dot(mb, wk_ref[...], preferred_element_type=F32).astype(BF16)
        v_ref[...] = jnp.dot(mb, wv_ref[...], preferred_element_type=F32).astype(BF16)

    bucket_ids = lax.broadcasted_iota(jnp.int32, (cnt_ref.shape[0], 1), 0).astype(F32)
    for half, (q_ref, a_ref) in enumerate(((q0_ref, a0_ref), (q1_ref, a1_ref))):
        rows = pl.ds(half * T, T)
        q = jnp.dot(x_ref[0, rows, :].astype(BF16), wq_ref[...], preferred_element_type=F32)
        q_ref[...] = (q * (XATTN_DH ** -0.5)).astype(BF16)
        for h in range(XATTN_HEADS):
            cols = slice(h * XATTN_DH, (h + 1) * XATTN_DH)
            s = lax.dot_general(q_ref[:, cols], k_ref[:, cols], (((1,), (1,)), ((), ())),
                                preferred_element_type=F32)
            e = jnp.exp(s - jnp.max(s, axis=-1, keepdims=True))
            p = e / jnp.sum(e, axis=-1, keepdims=True)
            a_ref[:, cols] = jnp.dot(p.astype(BF16), v_ref[:, cols],
                                     preferred_element_type=F32).astype(BF16)
        hres = jnp.dot(a_ref[...], wo_ref[...], preferred_element_type=F32)
        x2 = _layer_norm(DN_ALPHA * x_ref[0, rows, :] + hres, lng_ref[...], lnb_ref[...])
        o_ref[0, rows, 0:D_MODEL] = x2

        lt = lax.dot_general(wr_ref[...], x2.astype(BF16), (((1,), (1,)), ((), ())),
                             preferred_element_type=F32) + br_ref[...]
        bucket, g_lo, g_hi = _route([lt[i:i + 1, :] for i in range(N_EXPERTS)])
        route_ref[0, 0:1, rows] = bucket
        cnt_ref[...] += jnp.sum((bucket == bucket_ids).astype(F32), axis=1, keepdims=True)
        rt = jnp.concatenate([bucket, g_lo, g_hi, jnp.zeros((LANES - 3, T), F32)], axis=0)
        o_ref[0, rows, D_MODEL:D_EXT] = rt.T


def _xattn(x, mem, layer, wq, wk, wv, wo, lng, lnb, wr_t, br_col):
    B, S, D = x.shape
    T = TOK_TILE
    nt = S // (2 * T)
    tile = pl.BlockSpec((1, 2 * T, D), lambda b, s: (b, s, 0))
    proj = pl.BlockSpec((None, D, D), lambda b, s: (layer, 0, 0), pipeline_mode=pl.Buffered(1))
    consts = (lng, lnb, wr_t, br_col)
    return pl.pallas_call(
        _xattn_kernel,
        grid=(B, nt),
        in_specs=[tile, pl.BlockSpec((1, N_MEM, D), lambda b, s: (b, 0, 0))] + [proj] * 4
        + [_const_spec(c.shape, single_buffer=True) for c in consts],
        out_specs=[pl.BlockSpec((1, 2 * T, D_EXT), lambda b, s: (b, s, 0)),
                   pl.BlockSpec((1, 1, 2 * T), lambda b, s: (b * nt + s, 0, 0)),
                   pl.BlockSpec((CNT_ROWS, LANES), lambda b, s: (0, 0))],
        out_shape=[jax.ShapeDtypeStruct((B, S, D_EXT), F32),
                   jax.ShapeDtypeStruct((B * nt, 1, 2 * T), F32),
                   jax.ShapeDtypeStruct((CNT_ROWS, LANES), F32)],
        scratch_shapes=[
            pltpu.VMEM((D, D), BF16),
            pltpu.VMEM((D, D), BF16),
            pltpu.VMEM((D, D), BF16),
            pltpu.VMEM((D, D), BF16),
            pltpu.VMEM((N_MEM, D), BF16),
            pltpu.VMEM((N_MEM, D), BF16),
            pltpu.VMEM((T, D), BF16),
            pltpu.VMEM((T, D), BF16),
            pltpu.VMEM((T, D), BF16),
            pltpu.VMEM((T, D), BF16),
        ],
        compiler_params=pltpu.CompilerParams(
            dimension_semantics=("arbitrary", "arbitrary"), vmem_limit_bytes=VMEM_LIMIT),
        name="xattn",
    )(x, mem, wq, wk, wv, wo, *consts)


def _moe_kernel(elo_ref, ehi_ref, nrows_ref, start_ref, order_ref, x_hbm,
                wg_lo, wu_lo, wd_lo, wg_hi, wu_hi, wd_hi, lng_ref, lnb_ref, o_hbm,
                xbuf, obuf, gsem, ssem):
    del elo_ref, ehi_ref
    i = pl.program_id(0)
    n_tiles = pl.num_programs(0)
    tm = xbuf.shape[0] * SUBLANES
    slot = lax.rem(i, 2)
    nrows = nrows_ref[i]
    nxt = jnp.minimum(i + 1, n_tiles - 1)
    has_next = jnp.logical_and(i + 1 < n_tiles, nrows_ref[nxt] > 0)

    def gather_row(base, j, k):
        t = order_ref[base + j * SUBLANES + k]
        pltpu.make_async_copy(x_hbm.at[pl.ds(t, 1)], xbuf.at[j, pl.ds(k, 1)], gsem.at[0]).start(k % 2)

    def scatter_row(base, s, j, k):
        t = order_ref[base + j * SUBLANES + k]
        pltpu.make_async_copy(obuf.at[s, j, pl.ds(k, 1)], o_hbm.at[pl.ds(t, 1)], ssem.at[s]).start(k % 2)

    def wait_gather():
        pltpu.make_async_copy(xbuf, xbuf, gsem.at[0]).wait()

    def wait_scatter(s, n):
        @pl.when(n == tm)
        def _():
            pltpu.make_async_copy(obuf.at[s], obuf.at[s], ssem.at[s]).wait()

        @pl.when(n < tm)
        def _():
            def body(r, c):
                pltpu.make_async_copy(obuf.at[s, 0, pl.ds(0, 1)], obuf.at[s, 0, pl.ds(0, 1)],
                                      ssem.at[s]).wait()
                return c
            lax.fori_loop(0, n, body, 0)

    @pl.when(jnp.logical_and(i == 0, nrows > 0))
    def _():
        base0 = start_ref[0]

        def body(j, c):
            for k in range(SUBLANES):
                gather_row(base0, j, k)
            return c
        lax.fori_loop(0, tm // SUBLANES, body, 0)

    @pl.when(nrows > 0)
    def _():
        wait_gather()
        xe = xbuf[...].reshape(tm, xbuf.shape[2])
        x = xe[:, 0:D_MODEL]
        g_lo = xe[:, COL_GLO:COL_GLO + 1]
        g_hi = xe[:, COL_GHI:COL_GHI + 1]
        xb = x.astype(BF16)

        def expert(wg, wu, wd):
            hg = jnp.dot(xb, wg[0], preferred_element_type=F32)
            hu = jnp.dot(xb, wu[0], preferred_element_type=F32)
            return jnp.dot((_silu(hg) * hu).astype(BF16), wd[0], preferred_element_type=F32)

        y = g_lo * expert(wg_lo, wu_lo, wd_lo)
        y = y + g_hi * expert(wg_hi, wu_hi, wd_hi)
        res = _layer_norm(DN_ALPHA * x + y, lng_ref[...], lnb_ref[...])
        base_n = start_ref[nxt]
        for r in range(tm):
            gather_row(base_n, r // SUBLANES, r % SUBLANES)
        obuf[slot] = res.reshape(tm // SUBLANES, SUBLANES, D_MODEL)

        base = start_ref[i]
        for s in range(2):
            @pl.when(jnp.logical_and(nrows == tm, slot == s))
            def _(s=s):
                for r in range(tm):
                    scatter_row(base, s, r // SUBLANES, r % SUBLANES)

        @pl.when(nrows < tm)
        def _():
            def body(r, c):
                t = order_ref[base + r]
                pltpu.make_async_copy(obuf.at[slot, r // SUBLANES, pl.ds(r % SUBLANES, 1)],
                                      o_hbm.at[pl.ds(t, 1)], ssem.at[slot]).start()
                return c
            lax.fori_loop(0, nrows, body, 0)

        @pl.when(i > 0)
        def _():
            wait_scatter(1 - slot, nrows_ref[jnp.maximum(i - 1, 0)])

        @pl.when(jnp.logical_not(has_next))
        def _():
            wait_scatter(slot, nrows)
            wait_gather()


def _moe(x2e, order_pad, layer, wg, wu, wd, lng, lnb, tile_elo, tile_ehi, tile_nrows, tile_start):
    n_tok, de = x2e.shape
    n_tiles = tile_start.shape[0]
    tm, D = MOE_TILE, D_MODEL
    wlo = lambda s: pl.BlockSpec((None, 1) + s, lambda i, elo, ehi, nr, st, od: (layer, elo[i], 0, 0))
    whi = lambda s: pl.BlockSpec((None, 1) + s, lambda i, elo, ehi, nr, st, od: (layer, ehi[i], 0, 0))
    cst = lambda s: pl.BlockSpec(s, lambda i, elo, ehi, nr, st, od: (0, 0))
    gu, dn = (D, D_EXPERT), (D_EXPERT, D)
    grid_spec = pltpu.PrefetchScalarGridSpec(
        num_scalar_prefetch=5,
        grid=(n_tiles,),
        in_specs=[pl.BlockSpec(memory_space=pl.ANY),
                  wlo(gu), wlo(gu), wlo(dn), whi(gu), whi(gu), whi(dn),
                  cst(lng.shape), cst(lnb.shape)],
        out_specs=pl.BlockSpec(memory_space=pl.ANY),
        scratch_shapes=[
            pltpu.VMEM((tm // SUBLANES, SUBLANES, de), F32),
            pltpu.VMEM((2, tm // SUBLANES, SUBLANES, D), F32),
            pltpu.SemaphoreType.DMA((1,)),
            pltpu.SemaphoreType.DMA((2,)),
        ],
    )
    return pl.pallas_call(
        _moe_kernel,
        grid_spec=grid_spec,
        out_shape=jax.ShapeDtypeStruct((n_tok, D), F32),
        compiler_params=pltpu.CompilerParams(
            dimension_semantics=("arbitrary",), vmem_limit_bytes=VMEM_LIMIT),
        name="moe",
    )(tile_elo, tile_ehi, tile_nrows, tile_start, order_pad, x2e, wg, wu, wd, wg, wu, wd, lng, lnb)


def _dispatch_plan(bucket, counts, n_tok):
    tm = MOE_TILE
    n_tiles = n_tok // tm + N_BUCKETS
    assert n_tok <= 1 << 16
    keys = bucket * (1 << 16) + jnp.arange(n_tok, dtype=jnp.int32)
    order = lax.sort(keys, is_stable=False) & ((1 << 16) - 1)
    tiles_per = (counts + tm - 1) // tm
    tile_end = jnp.cumsum(tiles_per)
    tile_start = tile_end - tiles_per
    sorted_start = jnp.cumsum(counts) - counts
    n_valid = tile_end[-1]
    ti = jnp.arange(n_tiles, dtype=jnp.int32)
    valid = ti < n_valid
    in_bucket = lambda t: jnp.logical_and(t >= tile_start[None, :], t < tile_end[None, :]).astype(jnp.int32)
    member = in_bucket(ti[:, None])
    last_member = in_bucket(jnp.reshape(n_valid - 1, (1, 1)))
    pick = lambda table, m: jnp.sum(m * table[None, :], axis=1)
    e_lo = jnp.asarray([b // len(PAIRS) * EXPERTS_PER_GROUP + PAIRS[b % len(PAIRS)][0]
                        for b in range(N_BUCKETS)], jnp.int32)
    e_hi = jnp.asarray([b // len(PAIRS) * EXPERTS_PER_GROUP + PAIRS[b % len(PAIRS)][1]
                        for b in range(N_BUCKETS)], jnp.int32)
    tile_elo = jnp.where(valid, pick(e_lo, member), pick(e_lo, last_member))
    tile_ehi = jnp.where(valid, pick(e_hi, member), pick(e_hi, last_member))
    first_row = (ti - pick(tile_start, member)) * tm
    tile_nrows = jnp.where(valid, jnp.clip(pick(counts, member) - first_row, 0, tm), 0)
    start = jnp.where(valid, jnp.clip(pick(sorted_start, member) + first_row, 0, n_tok), n_tok)
    order_pad = jnp.concatenate([order, jnp.zeros((tm,), jnp.int32)])
    return (order_pad, tile_elo.astype(jnp.int32), tile_ehi.astype(jnp.int32),
            tile_nrows.astype(jnp.int32), start.astype(jnp.int32))


def kernel(x, mem, w_in, w_a2, b_a, gla_norm_g, w_s, b_s, sgu_ln_g, sgu_ln_b, w_out, wq_x, wk_x, wv_x,
           wo_x, w_router, b_router, w_gate, w_up, w_down, ln_g, ln_b):
    B, S, D = x.shape
    n_tok = B * S

    segs = (slice(0, 256), slice(256, 512), slice(512, 1024), slice(1024, 1536),
            slice(1552, 2064), slice(2064, 2576))
    w_in_r = jnp.concatenate(
        [w_in[:, :, s] for s in segs]
        + [jnp.pad(w_in[:, :, 1536:1552], ((0, 0), (0, 0), (0, A_PAD - GATE_RANK)))], axis=-1).astype(BF16)
    w_a2_p = jnp.pad(w_a2, ((0, 0), (0, A_PAD - GATE_RANK), (0, 0))).astype(BF16)
    bs_full = jnp.broadcast_to(b_s[:, :, :, None], b_s.shape + (SGU_CH,))
    w_out_b = w_out.astype(BF16)
    wg_b, wu_b, wd_b = (w.astype(BF16) for w in (w_gate, w_up, w_down))
    wr_t = w_router.T.astype(BF16)
    br_col = jnp.broadcast_to(b_router[:, None], (N_EXPERTS, TOK_TILE))
    row = lambda v: v.reshape(1, -1)

    for l in range(DEPTH):
        x = _mixer(x, w_in_r[l], w_a2_p[l], row(b_a[l]), row(gla_norm_g[l]), w_s[l], bs_full[l],
                   row(sgu_ln_g[l]), row(sgu_ln_b[l]), w_out_b[l], row(ln_g[l, 0]), row(ln_b[l, 0]))
        x2e, route, cnt = _xattn(x, mem, l, wq_x, wk_x, wv_x, wo_x, row(ln_g[l, 1]), row(ln_b[l, 1]),
                                 wr_t, br_col)
        bucket = route.reshape(-1).astype(jnp.int32)
        counts = cnt[:N_BUCKETS, 0].astype(jnp.int32)
        order_pad, t_elo, t_ehi, t_nrows, t_start = _dispatch_plan(bucket, counts, n_tok)
        x = _moe(x2e.reshape(n_tok, D_EXT), order_pad, l, wg_b, wu_b, wd_b,
                 row(ln_g[l, 2]), row(ln_b[l, 2]), t_elo, t_ehi, t_nrows, t_start).reshape(B, S, D)
    return x
```

```python
import functools

import jax
import jax.numpy as jnp
from jax import lax
from jax.experimental import pallas as pl
from jax.experimental.pallas import tpu as pltpu

F32 = jnp.float32
BF16 = jnp.bfloat16

D_MODEL = 1024
DEPTH = 2
CHUNK = 64
N_MEM = 256
D_GLA = 512
D_SGU = 512
GLA_HEADS = 4
D_QK = 256
GLA_DK = 64
GLA_DV = 128
GATE_RANK = 16
GATE_TEMP = 16.0
SGU_GROUPS = 4
SGU_BLOCK = 128
SGU_CH = 128
XATTN_HEADS = 4
XATTN_DH = 256
N_EXPERTS = 16
N_EXPERT_GROUPS = 4
EXPERTS_PER_GROUP = 4
D_EXPERT = 512
DN_ALPHA = (2.0 * DEPTH) ** 0.25
LN_EPS = 1e-5
RMS_EPS = 1e-6

LANES = 128
SUBLANES = 8
A_PAD = LANES
C_Q, C_K, C_V, C_R = 0, 256, 512, 1024
C_U, C_VS, C_A = 1536, 2048, 2560
D_INP = C_A + A_PAD
D_EXT = D_MODEL + LANES
COL_GLO, COL_GHI = D_MODEL + 1, D_MODEL + 2

PAIRS = ((0, 1), (0, 2), (0, 3), (1, 2), (1, 3), (2, 3))
N_BUCKETS = N_EXPERT_GROUPS * len(PAIRS)
CNT_ROWS = 32

TOK_TILE = 512
MOE_TILE = 512
VMEM_LIMIT = 56 * 1024 * 1024


def _layer_norm(x, g, b):
    mu = jnp.mean(x, axis=-1, keepdims=True)
    xc = x - mu
    var = jnp.mean(xc * xc, axis=-1, keepdims=True)
    return xc * lax.rsqrt(var + LN_EPS) * g + b


def _gelu_tanh(x):
    return 0.5 * x * (1.0 + jnp.tanh(0.7978845608028654 * (x + 0.044715 * (x * x * x))))


def _silu(x):
    return x / (1.0 + jnp.exp(-x))


def _mixer_kernel(x_ref, xn_ref, win_ref, wa2_ref, ba_ref, gng_ref, ws_ref, bs_ref, slg_ref, slb_ref,
                  wout_ref, lng_ref, lnb_ref, o_ref, st_ref, dec_ref, upd_ref, snap_ref, p0_ref, p1_ref,
                  g0_ref, g1_ref, y0_ref, y1_ref, *, steps_per_seq):
    T = xn_ref.shape[0]
    g = pl.program_id(0)
    consts = (wa2_ref, ba_ref, gng_ref, ws_ref, bs_ref, slg_ref, slb_ref, wout_ref, lng_ref, lnb_ref)
    gla_scratch = (st_ref, dec_ref, upd_ref, snap_ref)

    @pl.when(lax.rem(g, steps_per_seq) == 0)
    def _():
        st_ref[...] = jnp.zeros_like(st_ref)

    @pl.when(g == 0)
    def _():
        p0_ref[...] = jnp.dot(x_ref[pl.ds(0, T), :].astype(BF16), win_ref[...], preferred_element_type=F32)

    def in_proj(x):
        return jnp.dot(x.astype(BF16), win_ref[...], preferred_element_type=F32)

    lo, hi = pl.ds(0, T), pl.ds(T, T)
    p1_ref[...] = in_proj(x_ref[hi, :])
    o_ref[lo, :] = _mixer_gate_stage(x_ref.at[lo, :], p0_ref, *gla_scratch, g0_ref, y0_ref, *consts)
    p0_ref[...] = in_proj(xn_ref[...])
    o_ref[hi, :] = _mixer_gate_stage(x_ref.at[hi, :], p1_ref, *gla_scratch, g1_ref, y1_ref, *consts)


def _mixer_gate_stage(x_ref, p_ref, st_ref, dec_ref, upd_ref, snap_ref, g_ref, y_ref, wa2_ref, ba_ref, gng_ref,
                      ws_ref, bs_ref, slg_ref, slb_ref, wout_ref, lng_ref, lnb_ref):
    T = p_ref.shape[0]

    z = jnp.dot(p_ref[:, C_A:C_A + A_PAD].astype(BF16), wa2_ref[...],
                preferred_element_type=F32) + ba_ref[...]
    g_ref[...] = (jnp.minimum(z, 0.0) - jnp.log1p(jnp.exp(-jnp.abs(z)))) * (1.0 / GATE_TEMP)

    ri = lax.broadcasted_iota(jnp.int32, (CHUNK, 3 * CHUNK), 0)
    ci = lax.broadcasted_iota(jnp.int32, (CHUNK, 3 * CHUNK), 1) % CHUNK
    tri3 = (ci <= ri).astype(BF16)

    def cumsum_chunk(gc):
        hi = gc.astype(BF16)
        r1 = gc - hi.astype(F32)
        mid = r1.astype(BF16)
        lo = (r1 - mid.astype(F32)).astype(BF16)
        return jnp.dot(tri3, jnp.concatenate([hi, mid, lo], axis=0), preferred_element_type=F32)
    hr = lax.broadcasted_iota(jnp.int32, (D_GLA, D_QK), 0) // GLA_DV
    hc = lax.broadcasted_iota(jnp.int32, (D_GLA, D_QK), 1) // GLA_DK
    head_mask = (hr == hc).astype(F32)

    n_chunks = T // CHUNK
    for c in range(n_chunks):
        rows = pl.ds(c * CHUNK, CHUNK)
        bcum = cumsum_chunk(g_ref[rows, :])
        b_last = bcum[CHUNK - 1:CHUNK, :]
        dec_ref[c] = jnp.exp(b_last)
        k_dec = p_ref[rows, C_K:C_K + D_QK] * jnp.exp(b_last - bcum)
        v_c = p_ref[rows, C_V:C_V + D_GLA]
        upd = lax.dot_general(v_c.astype(BF16), k_dec.astype(BF16), (((0,), (0,)), ((), ())),
                              preferred_element_type=F32)
        upd_ref[c] = upd * head_mask
    for c in range(n_chunks):
        st = st_ref[...] * dec_ref[c] + upd_ref[c]
        st_ref[...] = st
        snap_ref[c] = st.astype(BF16)
    for c in range(n_chunks):
        rows = pl.ds(c * CHUNK, CHUNK)
        q_c = p_ref[rows, C_Q:C_Q + D_QK] * (GLA_DK ** -0.5)
        o_c = lax.dot_general(q_c.astype(BF16), snap_ref[c], (((1,), (1,)), ((), ())),
                              preferred_element_type=F32)
        r_c = p_ref[rows, C_R:C_R + D_GLA]
        for h in range(GLA_HEADS):
            cols = slice(h * GLA_DV, (h + 1) * GLA_DV)
            oh = o_c[:, cols]
            oh = oh * lax.rsqrt(jnp.mean(oh * oh, axis=-1, keepdims=True) + RMS_EPS)
            y_ref[rows, cols] = (oh * gng_ref[:, cols] * _silu(r_c[:, cols])).astype(BF16)

    ti = lax.broadcasted_iota(jnp.int32, (SGU_BLOCK, SGU_BLOCK), 0) // CHUNK
    si = lax.broadcasted_iota(jnp.int32, (SGU_BLOCK, SGU_BLOCK), 1) // CHUNK
    causal = si <= ti
    for gi in range(SGU_GROUPS):
        w_m = jnp.where(causal, ws_ref[gi], 0.0).astype(BF16)
        cu = slice(C_U + gi * SGU_CH, C_U + (gi + 1) * SGU_CH)
        cv = slice(C_VS + gi * SGU_CH, C_VS + (gi + 1) * SGU_CH)
        cg = slice(gi * SGU_CH, (gi + 1) * SGU_CH)
        n_blocks = T // SGU_BLOCK
        vn = [_layer_norm(_gelu_tanh(p_ref[pl.ds(n * SGU_BLOCK, SGU_BLOCK), cv]),
                          slg_ref[:, cg], slb_ref[:, cg]).astype(BF16) for n in range(n_blocks)]
        mixed = jnp.dot(w_m, jnp.concatenate(vn, axis=1), preferred_element_type=F32)
        for n in range(n_blocks):
            rows = pl.ds(n * SGU_BLOCK, SGU_BLOCK)
            u = _gelu_tanh(p_ref[rows, cu])
            m_n = mixed[:, n * SGU_CH:(n + 1) * SGU_CH] + bs_ref[gi]
            y_ref[rows, D_GLA + gi * SGU_CH:D_GLA + (gi + 1) * SGU_CH] = (u * m_n).astype(BF16)

    h = jnp.dot(y_ref[...], wout_ref[...], preferred_element_type=F32)
    return _layer_norm(DN_ALPHA * x_ref[...] + h, lng_ref[...], lnb_ref[...])


def _const_spec(shape, single_buffer=False):
    nd = len(shape)
    if single_buffer:
        return pl.BlockSpec(shape, lambda *_: (0,) * nd, pipeline_mode=pl.Buffered(1))
    return pl.BlockSpec(shape, lambda *_: (0,) * nd)


def _layer_spec(stacked, layer):
    nd = stacked.ndim
    return pl.BlockSpec((None,) + stacked.shape[1:], lambda *_: (layer,) + (0,) * (nd - 1),
                        pipeline_mode=pl.Buffered(1))


def _mixer(x, layer, win, wa2, ba, gng, ws, bs_full, slg, slb, wout, lng, lnb):
    B, S, D = x.shape
    T = TOK_TILE
    n_tiles = B * S // T
    assert S % (2 * T) == 0
    x2d = x.reshape(B * S, D)
    consts = (win, wa2, ba, gng, ws, bs_full, slg, slb, wout, lng, lnb)
    stacked = (True, True, False, False, True, True, False, False, True, False, False)
    pair = pl.BlockSpec((2 * T, D), lambda g: (g, 0))
    nxt = pl.BlockSpec((T, D), lambda g: (jnp.minimum(2 * g + 2, n_tiles - 1), 0))
    out = pl.pallas_call(
        functools.partial(_mixer_kernel, steps_per_seq=S // (2 * T)),
        grid=(n_tiles // 2,),
        in_specs=[pair, nxt] + [_layer_spec(c, layer) if st else _const_spec(c.shape, single_buffer=True)
                                for c, st in zip(consts, stacked)],
        out_specs=pair,
        out_shape=jax.ShapeDtypeStruct((B * S, D), F32),
        scratch_shapes=[
            pltpu.VMEM((D_GLA, D_QK), F32),
            pltpu.VMEM((T // CHUNK, 1, D_QK), F32),
            pltpu.VMEM((T // CHUNK, D_GLA, D_QK), F32),
            pltpu.VMEM((T // CHUNK, D_GLA, D_QK), BF16),
            pltpu.VMEM((T, D_INP), F32),
            pltpu.VMEM((T, D_INP), F32),
            pltpu.VMEM((T, D_QK), F32),
            pltpu.VMEM((T, D_QK), F32),
            pltpu.VMEM((T, D), BF16),
            pltpu.VMEM((T, D), BF16),
        ],
        compiler_params=pltpu.CompilerParams(
            dimension_semantics=("arbitrary",), vmem_limit_bytes=VMEM_LIMIT),
        name="mixer",
    )(x2d, x2d, *consts)
    return out.reshape(B, S, D)


def _route(logits):
    m = functools.reduce(jnp.maximum, logits)
    e = [jnp.exp(l - m) for l in logits]
    inv = 1.0 / functools.reduce(jnp.add, e)
    sc = [v * inv for v in e]

    def top2(a):
        first = functools.reduce(jnp.maximum, a)
        second = None
        for i in range(len(a)):
            for j in range(i + 1, len(a)):
                mn = jnp.minimum(a[i], a[j])
                second = mn if second is None else jnp.maximum(second, mn)
        return first + second

    gs = [top2(sc[g * EXPERTS_PER_GROUP:(g + 1) * EXPERTS_PER_GROUP]) for g in range(N_EXPERT_GROUPS)]
    best, g_sel = gs[0], jnp.zeros_like(gs[0])
    for g in range(1, N_EXPERT_GROUPS):
        better = gs[g] > best
        g_sel = jnp.where(better, float(g), g_sel)
        best = jnp.where(better, gs[g], best)
    a = []
    for j in range(EXPERTS_PER_GROUP):
        v = jnp.zeros_like(best)
        for g in range(N_EXPERT_GROUPS):
            v = v + jnp.where(g_sel == float(g), sc[g * EXPERTS_PER_GROUP + j], 0.0)
        a.append(v)
    w1, i1 = a[0], jnp.zeros_like(a[0])
    for j in range(1, EXPERTS_PER_GROUP):
        better = a[j] > w1
        i1 = jnp.where(better, float(j), i1)
        w1 = jnp.where(better, a[j], w1)
    w2, i2 = None, None
    for j in range(EXPERTS_PER_GROUP):
        cand = jnp.where(i1 == float(j), -1.0, a[j])
        if w2 is None:
            w2, i2 = cand, jnp.zeros_like(cand)
        else:
            better = cand > w2
            i2 = jnp.where(better, float(j), i2)
            w2 = jnp.where(better, cand, w2)
    tot = w1 + w2
    w1n, w2n = w1 / tot, w2 / tot
    first_is_lo = i1 < i2
    lo = jnp.where(first_is_lo, i1, i2)
    hi = jnp.where(first_is_lo, i2, i1)
    pair = jnp.zeros_like(lo)
    for pi, (pa, pb) in enumerate(PAIRS):
        pair = jnp.where((lo == float(pa)) & (hi == float(pb)), float(pi), pair)
    bucket = g_sel * float(len(PAIRS)) + pair
    g_lo = jnp.where(first_is_lo, w1n, w2n)
    g_hi = jnp.where(first_is_lo, w2n, w1n)
    return bucket, g_lo, g_hi


def _xattn_kernel(x_ref, mem_ref, wq32_ref, wk32_ref, wv32_ref, wo32_ref, lng_ref, lnb_ref, wr_ref, br_ref,
                  o_ref, route_ref, cnt_ref, wq_ref, wk_ref, wv_ref, wo_ref, k_ref, v_ref,
                  q0_ref, q1_ref, a0_ref, a1_ref):
    T = q0_ref.shape[0]

    @pl.when(jnp.logical_and(pl.program_id(0) == 0, pl.program_id(1) == 0))
    def _():
        cnt_ref[...] = jnp.zeros_like(cnt_ref)
        for w32_ref, w_ref in ((wq32_ref, wq_ref), (wk32_ref, wk_ref), (wv32_ref, wv_ref), (wo32_ref, wo_ref)):
            for r in range(0, w_ref.shape[0], LANES):
                w_ref[r:r + LANES, :] = w32_ref[r:r + LANES, :].astype(BF16)

    @pl.when(pl.program_id(1) == 0)
    def _():
        mb = mem_ref[0].astype(BF16)
        k_ref[...] = jnp.dot(mb, wk_ref[...], preferred_element_type=F32).astype(BF16)
        v_ref[...] = jnp.dot(mb, wv_ref[...], preferred_element_type=F32).astype(BF16)

    bucket_ids = lax.broadcasted_iota(jnp.int32, (cnt_ref.shape[0], 1), 0).astype(F32)
    for half, (q_ref, a_ref) in enumerate(((q0_ref, a0_ref), (q1_ref, a1_ref))):
        rows = pl.ds(half * T, T)
        q = jnp.dot(x_ref[0, rows, :].astype(BF16), wq_ref[...], preferred_element_type=F32)
        q_ref[...] = (q * (XATTN_DH ** -0.5)).astype(BF16)
        for h in range(XATTN_HEADS):
            cols = slice(h * XATTN_DH, (h + 1) * XATTN_DH)
            s = lax.dot_general(q_ref[:, cols], k_ref[:, cols], (((1,), (1,)), ((), ())),
                                preferred_element_type=F32)
            e = jnp.exp(s - jnp.max(s, axis=-1, keepdims=True))
            p = e / jnp.sum(e, axis=-1, keepdims=True)
            a_ref[:, cols] = jnp.dot(p.astype(BF16), v_ref[:, cols],
                                     preferred_element_type=F32).astype(BF16)
        hres = jnp.dot(a_ref[...], wo_ref[...], preferred_element_type=F32)
        x2 = _layer_norm(DN_ALPHA * x_ref[0, rows, :] + hres, lng_ref[...], lnb_ref[...])
        o_ref[0, rows, 0:D_MODEL] = x2

        lt = lax.dot_general(wr_ref[...], x2.astype(BF16), (((1,), (1,)), ((), ())),
                             preferred_element_type=F32) + br_ref[...]
        bucket, g_lo, g_hi = _route([lt[i:i + 1, :] for i in range(N_EXPERTS)])
        route_ref[0, 0:1, rows] = bucket
        cnt_ref[...] += jnp.sum((bucket == bucket_ids).astype(F32), axis=1, keepdims=True)
        rt = jnp.concatenate([bucket, g_lo, g_hi, jnp.zeros((LANES - 3, T), F32)], axis=0)
        o_ref[0, rows, D_MODEL:D_EXT] = rt.T


def _xattn(x, mem, layer, wq, wk, wv, wo, lng, lnb, wr_t, br_col):
    B, S, D = x.shape
    T = TOK_TILE
    nt = S // (2 * T)
    tile = pl.BlockSpec((1, 2 * T, D), lambda b, s: (b, s, 0))
    proj = pl.BlockSpec((None, D, D), lambda b, s: (layer, 0, 0), pipeline_mode=pl.Buffered(1))
    consts = (lng, lnb, wr_t, br_col)
    return pl.pallas_call(
        _xattn_kernel,
        grid=(B, nt),
        in_specs=[tile, pl.BlockSpec((1, N_MEM, D), lambda b, s: (b, 0, 0))] + [proj] * 4
        + [_const_spec(c.shape, single_buffer=True) for c in consts],
        out_specs=[pl.BlockSpec((1, 2 * T, D_EXT), lambda b, s: (b, s, 0)),
                   pl.BlockSpec((1, 1, 2 * T), lambda b, s: (b * nt + s, 0, 0)),
                   pl.BlockSpec((CNT_ROWS, LANES), lambda b, s: (0, 0))],
        out_shape=[jax.ShapeDtypeStruct((B, S, D_EXT), F32),
                   jax.ShapeDtypeStruct((B * nt, 1, 2 * T), F32),
                   jax.ShapeDtypeStruct((CNT_ROWS, LANES), F32)],
        scratch_shapes=[
            pltpu.VMEM((D, D), BF16),
            pltpu.VMEM((D, D), BF16),
            pltpu.VMEM((D, D), BF16),
            pltpu.VMEM((D, D), BF16),
            pltpu.VMEM((N_MEM, D), BF16),
            pltpu.VMEM((N_MEM, D), BF16),
            pltpu.VMEM((T, D), BF16),
            pltpu.VMEM((T, D), BF16),
            pltpu.VMEM((T, D), BF16),
            pltpu.VMEM((T, D), BF16),
        ],
        compiler_params=pltpu.CompilerParams(
            dimension_semantics=("arbitrary", "arbitrary"), vmem_limit_bytes=VMEM_LIMIT),
        name="xattn",
    )(x, mem, wq, wk, wv, wo, *consts)


def _moe_kernel(elo_ref, ehi_ref, nrows_ref, start_ref, order_ref, x_hbm,
                wg_lo, wu_lo, wd_lo, wg_hi, wu_hi, wd_hi, lng_ref, lnb_ref, o_hbm,
                xbuf, obuf, gsem, ssem):
    del elo_ref, ehi_ref
    i = pl.program_id(0)
    n_tiles = pl.num_programs(0)
    tm = xbuf.shape[0] * SUBLANES
    slot = lax.rem(i, 2)
    nrows = nrows_ref[i]
    nxt = jnp.minimum(i + 1, n_tiles - 1)
    has_next = jnp.logical_and(i + 1 < n_tiles, nrows_ref[nxt] > 0)

    def gather_row(base, j, k):
        t = order_ref[base + j * SUBLANES + k]
        pltpu.make_async_copy(x_hbm.at[pl.ds(t, 1)], xbuf.at[j, pl.ds(k, 1)], gsem.at[0]).start(k % 2)

    def scatter_row(base, s, j, k):
        t = order_ref[base + j * SUBLANES + k]
        pltpu.make_async_copy(obuf.at[s, j, pl.ds(k, 1)], o_hbm.at[pl.ds(t, 1)], ssem.at[s]).start(k % 2)

    def wait_gather():
        pltpu.make_async_copy(xbuf, xbuf, gsem.at[0]).wait()

    def wait_scatter(s, n):
        @pl.when(n == tm)
        def _():
            pltpu.make_async_copy(obuf.at[s], obuf.at[s], ssem.at[s]).wait()

        @pl.when(n < tm)
        def _():
            def body(r, c):
                pltpu.make_async_copy(obuf.at[s, 0, pl.ds(0, 1)], obuf.at[s, 0, pl.ds(0, 1)],
                                      ssem.at[s]).wait()
                return c
            lax.fori_loop(0, n, body, 0)

    @pl.when(jnp.logical_and(i == 0, nrows > 0))
    def _():
        base0 = start_ref[0]

        def body(j, c):
            for k in range(SUBLANES):
                gather_row(base0, j, k)
            return c
        lax.fori_loop(0, tm // SUBLANES, body, 0)

    @pl.when(nrows > 0)
    def _():
        wait_gather()
        xe = xbuf[...].reshape(tm, xbuf.shape[2])
        x = xe[:, 0:D_MODEL]
        g_lo = xe[:, COL_GLO:COL_GLO + 1]
        g_hi = xe[:, COL_GHI:COL_GHI + 1]
        xb = x.astype(BF16)

        def expert(wg, wu, wd):
            hg = jnp.dot(xb, wg[0], preferred_element_type=F32)
            hu = jnp.dot(xb, wu[0], preferred_element_type=F32)
            return jnp.dot((_silu(hg) * hu).astype(BF16), wd[0], preferred_element_type=F32)

        y = g_lo * expert(wg_lo, wu_lo, wd_lo)
        y = y + g_hi * expert(wg_hi, wu_hi, wd_hi)
        res = _layer_norm(DN_ALPHA * x + y, lng_ref[...], lnb_ref[...])
        base_n = start_ref[nxt]
        for r in range(tm):
            gather_row(base_n, r // SUBLANES, r % SUBLANES)
        obuf[slot] = res.reshape(tm // SUBLANES, SUBLANES, D_MODEL)

        base = start_ref[i]
        for s in range(2):
            @pl.when(jnp.logical_and(nrows == tm, slot == s))
            def _(s=s):
                for r in range(tm):
                    scatter_row(base, s, r // SUBLANES, r % SUBLANES)

        @pl.when(nrows < tm)
        def _():
            def body(r, c):
                t = order_ref[base + r]
                pltpu.make_async_copy(obuf.at[slot, r // SUBLANES, pl.ds(r % SUBLANES, 1)],
                                      o_hbm.at[pl.ds(t, 1)], ssem.at[slot]).start()
                return c
            lax.fori_loop(0, nrows, body, 0)

        @pl.when(i > 0)
        def _():
            wait_scatter(1 - slot, nrows_ref[jnp.maximum(i - 1, 0)])

        @pl.when(jnp.logical_not(has_next))
        def _():
            wait_scatter(slot, nrows)
            wait_gather()


def _moe(x2e, order_pad, layer, wg, wu, wd, lng, lnb, tile_elo, tile_ehi, tile_nrows, tile_start):
    n_tok, de = x2e.shape
    n_tiles = tile_start.shape[0]
    tm, D = MOE_TILE, D_MODEL
    wlo = lambda s: pl.BlockSpec((None, 1) + s, lambda i, elo, ehi, nr, st, od: (layer, elo[i], 0, 0))
    whi = lambda s: pl.BlockSpec((None, 1) + s, lambda i, elo, ehi, nr, st, od: (layer, ehi[i], 0, 0))
    cst = lambda s: pl.BlockSpec(s, lambda i, elo, ehi, nr, st, od: (0, 0))
    gu, dn = (D, D_EXPERT), (D_EXPERT, D)
    grid_spec = pltpu.PrefetchScalarGridSpec(
        num_scalar_prefetch=5,
        grid=(n_tiles,),
        in_specs=[pl.BlockSpec(memory_space=pl.ANY),
                  wlo(gu), wlo(gu), wlo(dn), whi(gu), whi(gu), whi(dn),
                  cst(lng.shape), cst(lnb.shape)],
        out_specs=pl.BlockSpec(memory_space=pl.ANY),
        scratch_shapes=[
            pltpu.VMEM((tm // SUBLANES, SUBLANES, de), F32),
            pltpu.VMEM((2, tm // SUBLANES, SUBLANES, D), F32),
            pltpu.SemaphoreType.DMA((1,)),
            pltpu.SemaphoreType.DMA((2,)),
        ],
    )
    return pl.pallas_call(
        _moe_kernel,
        grid_spec=grid_spec,
        out_shape=jax.ShapeDtypeStruct((n_tok, D), F32),
        compiler_params=pltpu.CompilerParams(
            dimension_semantics=("arbitrary",), vmem_limit_bytes=VMEM_LIMIT),
        name="moe",
    )(tile_elo, tile_ehi, tile_nrows, tile_start, order_pad, x2e, wg, wu, wd, wg, wu, wd, lng, lnb)


def _dispatch_plan(bucket, counts, n_tok):
    tm = MOE_TILE
    n_tiles = n_tok // tm + N_BUCKETS
    assert n_tok <= 1 << 16
    keys = bucket * (1 << 16) + jnp.arange(n_tok, dtype=jnp.int32)
    order = lax.sort(keys, is_stable=False) & ((1 << 16) - 1)
    tiles_per = (counts + tm - 1) // tm
    tile_end = jnp.cumsum(tiles_per)
    tile_start = tile_end - tiles_per
    sorted_start = jnp.cumsum(counts) - counts
    n_valid = tile_end[-1]
    ti = jnp.arange(n_tiles, dtype=jnp.int32)
    valid = ti < n_valid
    in_bucket = lambda t: jnp.logical_and(t >= tile_start[None, :], t < tile_end[None, :]).astype(jnp.int32)
    member = in_bucket(ti[:, None])
    last_member = in_bucket(jnp.reshape(n_valid - 1, (1, 1)))
    pick = lambda table, m: jnp.sum(m * table[None, :], axis=1)
    e_lo = jnp.asarray([b // len(PAIRS) * EXPERTS_PER_GROUP + PAIRS[b % len(PAIRS)][0]
                        for b in range(N_BUCKETS)], jnp.int32)
    e_hi = jnp.asarray([b // len(PAIRS) * EXPERTS_PER_GROUP + PAIRS[b % len(PAIRS)][1]
                        for b in range(N_BUCKETS)], jnp.int32)
    tile_elo = jnp.where(valid, pick(e_lo, member), pick(e_lo, last_member))
    tile_ehi = jnp.where(valid, pick(e_hi, member), pick(e_hi, last_member))
    first_row = (ti - pick(tile_start, member)) * tm
    tile_nrows = jnp.where(valid, jnp.clip(pick(counts, member) - first_row, 0, tm), 0)
    start = jnp.where(valid, jnp.clip(pick(sorted_start, member) + first_row, 0, n_tok), n_tok)
    order_pad = jnp.concatenate([order, jnp.zeros((tm,), jnp.int32)])
    return (order_pad, tile_elo.astype(jnp.int32), tile_ehi.astype(jnp.int32),
            tile_nrows.astype(jnp.int32), start.astype(jnp.int32))


def kernel(x, mem, w_in, w_a2, b_a, gla_norm_g, w_s, b_s, sgu_ln_g, sgu_ln_b, w_out, wq_x, wk_x, wv_x,
           wo_x, w_router, b_router, w_gate, w_up, w_down, ln_g, ln_b):
    B, S, D = x.shape
    n_tok = B * S

    segs = (slice(0, 256), slice(256, 512), slice(512, 1024), slice(1024, 1536),
            slice(1552, 2064), slice(2064, 2576))
    w_in_r = jnp.concatenate(
        [w_in[:, :, s] for s in segs]
        + [jnp.pad(w_in[:, :, 1536:1552], ((0, 0), (0, 0), (0, A_PAD - GATE_RANK)))], axis=-1).astype(BF16)
    w_a2_p = jnp.pad(w_a2, ((0, 0), (0, A_PAD - GATE_RANK), (0, 0))).astype(BF16)
    bs_full = jnp.broadcast_to(b_s[:, :, :, None], b_s.shape + (SGU_CH,))
    w_out_b = w_out.astype(BF16)
    wg_b, wu_b, wd_b = (w.astype(BF16) for w in (w_gate, w_up, w_down))
    wr_t = w_router.T.astype(BF16)
    br_col = jnp.broadcast_to(b_router[:, None], (N_EXPERTS, TOK_TILE))
    row = lambda v: v.reshape(1, -1)

    for l in range(DEPTH):
        x = _mixer(x, l, w_in_r, w_a2_p, row(b_a[l]), row(gla_norm_g[l]), w_s, bs_full,
                   row(sgu_ln_g[l]), row(sgu_ln_b[l]), w_out_b, row(ln_g[l, 0]), row(ln_b[l, 0]))
        x2e, route, cnt = _xattn(x, mem, l, wq_x, wk_x, wv_x, wo_x, row(ln_g[l, 1]), row(ln_b[l, 1]),
                                 wr_t, br_col)
        bucket = route.reshape(-1).astype(jnp.int32)
        counts = cnt[:N_BUCKETS, 0].astype(jnp.int32)
        order_pad, t_elo, t_ehi, t_nrows, t_start = _dispatch_plan(bucket, counts, n_tok)
        x = _moe(x2e.reshape(n_tok, D_EXT), order_pad, l, wg_b, wu_b, wd_b,
                 row(ln_g[l, 2]), row(ln_b[l, 2]), t_elo, t_ehi, t_nrows, t_start).reshape(B, S, D)
    return x
```

```python
import functools

import jax
import jax.numpy as jnp
from jax import lax
from jax.experimental import pallas as pl
from jax.experimental.pallas import tpu as pltpu

F32 = jnp.float32
BF16 = jnp.bfloat16

D_MODEL = 1024
DEPTH = 2
CHUNK = 64
N_MEM = 256
D_GLA = 512
D_SGU = 512
GLA_HEADS = 4
D_QK = 256
GLA_DK = 64
GLA_DV = 128
GATE_RANK = 16
GATE_TEMP = 16.0
SGU_GROUPS = 4
SGU_BLOCK = 128
SGU_CH = 128
XATTN_HEADS = 4
XATTN_DH = 256
N_EXPERTS = 16
N_EXPERT_GROUPS = 4
EXPERTS_PER_GROUP = 4
D_EXPERT = 512
DN_ALPHA = (2.0 * DEPTH) ** 0.25
LN_EPS = 1e-5
RMS_EPS = 1e-6

LANES = 128
SUBLANES = 8
A_PAD = LANES
C_Q, C_K, C_V, C_R = 0, 256, 512, 1024
C_U, C_VS, C_A = 1536, 2048, 2560
D_INP = C_A + A_PAD
D_EXT = D_MODEL + LANES
COL_GLO, COL_GHI = D_MODEL + 1, D_MODEL + 2

PAIRS = ((0, 1), (0, 2), (0, 3), (1, 2), (1, 3), (2, 3))
N_BUCKETS = N_EXPERT_GROUPS * len(PAIRS)
CNT_ROWS = 32

TOK_TILE = 512
MOE_TILE = 512
VMEM_LIMIT = 56 * 1024 * 1024


def _layer_norm(x, g, b):
    mu = jnp.mean(x, axis=-1, keepdims=True)
    xc = x - mu
    var = jnp.mean(xc * xc, axis=-1, keepdims=True)
    return xc * lax.rsqrt(var + LN_EPS) * g + b


def _gelu_tanh(x):
    return 0.5 * x * (1.0 + jnp.tanh(0.7978845608028654 * (x + 0.044715 * (x * x * x))))


def _silu(x):
    return x / (1.0 + jnp.exp(-x))


def _mixer_kernel(x_ref, xn_ref, win_ref, wa2_ref, ba_ref, gng_ref, ws_ref, bs_ref, slg_ref, slb_ref,
                  wout_ref, lng_ref, lnb_ref, o_ref, st_ref, dec_ref, upd_ref, snap_ref, p0_ref, p1_ref,
                  g0_ref, g1_ref, y0_ref, y1_ref, *, steps_per_seq):
    T = xn_ref.shape[0]
    g = pl.program_id(0)
    consts = (wa2_ref, ba_ref, gng_ref, ws_ref, bs_ref, slg_ref, slb_ref, wout_ref, lng_ref, lnb_ref)
    gla_scratch = (st_ref, dec_ref, upd_ref, snap_ref)

    @pl.when(lax.rem(g, steps_per_seq) == 0)
    def _():
        st_ref[...] = jnp.zeros_like(st_ref)

    @pl.when(g == 0)
    def _():
        p0_ref[...] = jnp.dot(x_ref[pl.ds(0, T), :].astype(BF16), win_ref[...], preferred_element_type=F32)

    def in_proj(x):
        return jnp.dot(x.astype(BF16), win_ref[...], preferred_element_type=F32)

    lo, hi = pl.ds(0, T), pl.ds(T, T)
    p1_ref[...] = in_proj(x_ref[hi, :])
    o_ref[lo, :] = _mixer_gate_stage(x_ref.at[lo, :], p0_ref, *gla_scratch, g0_ref, y0_ref, *consts)
    p0_ref[...] = in_proj(xn_ref[...])
    o_ref[hi, :] = _mixer_gate_stage(x_ref.at[hi, :], p1_ref, *gla_scratch, g1_ref, y1_ref, *consts)


def _mixer_gate_stage(x_ref, p_ref, st_ref, dec_ref, upd_ref, snap_ref, g_ref, y_ref, wa2_ref, ba_ref, gng_ref,
                      ws_ref, bs_ref, slg_ref, slb_ref, wout_ref, lng_ref, lnb_ref):
    T = p_ref.shape[0]

    z = jnp.dot(p_ref[:, C_A:C_A + A_PAD].astype(BF16), wa2_ref[...],
                preferred_element_type=F32) + ba_ref[...]
    g_ref[...] = (jnp.minimum(z, 0.0) - jnp.log1p(jnp.exp(-jnp.abs(z)))) * (1.0 / GATE_TEMP)

    ri = lax.broadcasted_iota(jnp.int32, (CHUNK, 3 * CHUNK), 0)
    ci = lax.broadcasted_iota(jnp.int32, (CHUNK, 3 * CHUNK), 1) % CHUNK
    tri3 = (ci <= ri).astype(BF16)

    def cumsum_chunk(gc):
        hi = gc.astype(BF16)
        r1 = gc - hi.astype(F32)
        mid = r1.astype(BF16)
        lo = (r1 - mid.astype(F32)).astype(BF16)
        return jnp.dot(tri3, jnp.concatenate([hi, mid, lo], axis=0), preferred_element_type=F32)
    hr = lax.broadcasted_iota(jnp.int32, (D_GLA, D_QK), 0) // GLA_DV
    hc = lax.broadcasted_iota(jnp.int32, (D_GLA, D_QK), 1) // GLA_DK
    head_mask = (hr == hc).astype(F32)

    n_chunks = T // CHUNK
    for c in range(n_chunks):
        rows = pl.ds(c * CHUNK, CHUNK)
        bcum = cumsum_chunk(g_ref[rows, :])
        b_last = bcum[CHUNK - 1:CHUNK, :]
        dec_ref[c] = jnp.exp(b_last)
        k_dec = p_ref[rows, C_K:C_K + D_QK] * jnp.exp(b_last - bcum)
        v_c = p_ref[rows, C_V:C_V + D_GLA]
        upd = lax.dot_general(v_c.astype(BF16), k_dec.astype(BF16), (((0,), (0,)), ((), ())),
                              preferred_element_type=F32)
        upd_ref[c] = upd * head_mask
    for c in range(n_chunks):
        st = st_ref[...] * dec_ref[c] + upd_ref[c]
        st_ref[...] = st
        snap_ref[c] = st.astype(BF16)
    for c in range(n_chunks):
        rows = pl.ds(c * CHUNK, CHUNK)
        q_c = p_ref[rows, C_Q:C_Q + D_QK] * (GLA_DK ** -0.5)
        o_c = lax.dot_general(q_c.astype(BF16), snap_ref[c], (((1,), (1,)), ((), ())),
                              preferred_element_type=F32)
        r_c = p_ref[rows, C_R:C_R + D_GLA]
        for h in range(GLA_HEADS):
            cols = slice(h * GLA_DV, (h + 1) * GLA_DV)
            oh = o_c[:, cols]
            oh = oh * lax.rsqrt(jnp.mean(oh * oh, axis=-1, keepdims=True) + RMS_EPS)
            y_ref[rows, cols] = (oh * gng_ref[:, cols] * _silu(r_c[:, cols])).astype(BF16)

    ti = lax.broadcasted_iota(jnp.int32, (SGU_BLOCK, SGU_BLOCK), 0) // CHUNK
    si = lax.broadcasted_iota(jnp.int32, (SGU_BLOCK, SGU_BLOCK), 1) // CHUNK
    causal = si <= ti
    for gi in range(SGU_GROUPS):
        w_m = jnp.where(causal, ws_ref[gi], 0.0).astype(BF16)
        cu = slice(C_U + gi * SGU_CH, C_U + (gi + 1) * SGU_CH)
        cv = slice(C_VS + gi * SGU_CH, C_VS + (gi + 1) * SGU_CH)
        cg = slice(gi * SGU_CH, (gi + 1) * SGU_CH)
        n_blocks = T // SGU_BLOCK
        vn = [_layer_norm(_gelu_tanh(p_ref[pl.ds(n * SGU_BLOCK, SGU_BLOCK), cv]),
                          slg_ref[:, cg], slb_ref[:, cg]).astype(BF16) for n in range(n_blocks)]
        mixed = jnp.dot(w_m, jnp.concatenate(vn, axis=1), preferred_element_type=F32)
        for n in range(n_blocks):
            rows = pl.ds(n * SGU_BLOCK, SGU_BLOCK)
            u = _gelu_tanh(p_ref[rows, cu])
            m_n = mixed[:, n * SGU_CH:(n + 1) * SGU_CH] + bs_ref[gi]
            y_ref[rows, D_GLA + gi * SGU_CH:D_GLA + (gi + 1) * SGU_CH] = (u * m_n).astype(BF16)

    h = jnp.dot(y_ref[...], wout_ref[...], preferred_element_type=F32)
    return _layer_norm(DN_ALPHA * x_ref[...] + h, lng_ref[...], lnb_ref[...])


def _const_spec(shape, single_buffer=False):
    nd = len(shape)
    if single_buffer:
        return pl.BlockSpec(shape, lambda *_: (0,) * nd, pipeline_mode=pl.Buffered(1))
    return pl.BlockSpec(shape, lambda *_: (0,) * nd)


def _layer_spec(stacked, layer):
    nd = stacked.ndim
    return pl.BlockSpec((None,) + stacked.shape[1:], lambda *_: (layer,) + (0,) * (nd - 1),
                        pipeline_mode=pl.Buffered(1))


def _mixer(x, layer, win, wa2, ba, gng, ws, bs_full, slg, slb, wout, lng, lnb):
    B, S, D = x.shape
    T = TOK_TILE
    n_tiles = B * S // T
    assert S % (2 * T) == 0
    x2d = x.reshape(B * S, D)
    consts = (win, wa2, ba, gng, ws, bs_full, slg, slb, wout, lng, lnb)
    stacked = (True, True, False, False, True, True, False, False, True, False, False)
    pair = pl.BlockSpec((2 * T, D), lambda g: (g, 0))
    nxt = pl.BlockSpec((T, D), lambda g: (jnp.minimum(2 * g + 2, n_tiles - 1), 0))
    out = pl.pallas_call(
        functools.partial(_mixer_kernel, steps_per_seq=S // (2 * T)),
        grid=(n_tiles // 2,),
        in_specs=[pair, nxt] + [_layer_spec(c, layer) if st else _const_spec(c.shape, single_buffer=True)
                                for c, st in zip(consts, stacked)],
        out_specs=pair,
        out_shape=jax.ShapeDtypeStruct((B * S, D), F32),
        scratch_shapes=[
            pltpu.VMEM((D_GLA, D_QK), F32),
            pltpu.VMEM((T // CHUNK, 1, D_QK), F32),
            pltpu.VMEM((T // CHUNK, D_GLA, D_QK), F32),
            pltpu.VMEM((T // CHUNK, D_GLA, D_QK), BF16),
            pltpu.VMEM((T, D_INP), F32),
            pltpu.VMEM((T, D_INP), F32),
            pltpu.VMEM((T, D_QK), F32),
            pltpu.VMEM((T, D_QK), F32),
            pltpu.VMEM((T, D), BF16),
            pltpu.VMEM((T, D), BF16),
        ],
        compiler_params=pltpu.CompilerParams(
            dimension_semantics=("arbitrary",), vmem_limit_bytes=VMEM_LIMIT),
        name="mixer",
    )(x2d, x2d, *consts)
    return out.reshape(B, S, D)


def _route(logits):
    m = functools.reduce(jnp.maximum, logits)
    e = [jnp.exp(l - m) for l in logits]
    inv = 1.0 / functools.reduce(jnp.add, e)
    sc = [v * inv for v in e]

    def top2(a):
        first = functools.reduce(jnp.maximum, a)
        second = None
        for i in range(len(a)):
            for j in range(i + 1, len(a)):
                mn = jnp.minimum(a[i], a[j])
                second = mn if second is None else jnp.maximum(second, mn)
        return first + second

    gs = [top2(sc[g * EXPERTS_PER_GROUP:(g + 1) * EXPERTS_PER_GROUP]) for g in range(N_EXPERT_GROUPS)]
    best, g_sel = gs[0], jnp.zeros_like(gs[0])
    for g in range(1, N_EXPERT_GROUPS):
        better = gs[g] > best
        g_sel = jnp.where(better, float(g), g_sel)
        best = jnp.where(better, gs[g], best)
    a = []
    for j in range(EXPERTS_PER_GROUP):
        v = jnp.zeros_like(best)
        for g in range(N_EXPERT_GROUPS):
            v = v + jnp.where(g_sel == float(g), sc[g * EXPERTS_PER_GROUP + j], 0.0)
        a.append(v)
    w1, i1 = a[0], jnp.zeros_like(a[0])
    for j in range(1, EXPERTS_PER_GROUP):
        better = a[j] > w1
        i1 = jnp.where(better, float(j), i1)
        w1 = jnp.where(better, a[j], w1)
    w2, i2 = None, None
    for j in range(EXPERTS_PER_GROUP):
        cand = jnp.where(i1 == float(j), -1.0, a[j])
        if w2 is None:
            w2, i2 = cand, jnp.zeros_like(cand)
        else:
            better = cand > w2
            i2 = jnp.where(better, float(j), i2)
            w2 = jnp.where(better, cand, w2)
    tot = w1 + w2
    w1n, w2n = w1 / tot, w2 / tot
    first_is_lo = i1 < i2
    lo = jnp.where(first_is_lo, i1, i2)
    hi = jnp.where(first_is_lo, i2, i1)
    pair = jnp.zeros_like(lo)
    for pi, (pa, pb) in enumerate(PAIRS):
        pair = jnp.where((lo == float(pa)) & (hi == float(pb)), float(pi), pair)
    bucket = g_sel * float(len(PAIRS)) + pair
    g_lo = jnp.where(first_is_lo, w1n, w2n)
    g_hi = jnp.where(first_is_lo, w2n, w1n)
    return bucket, g_lo, g_hi


def _xattn_kernel(x_ref, mem_ref, wq32_ref, wk32_ref, wv32_ref, wo32_ref, lng_ref, lnb_ref, wr_ref, br_ref,
                  o_ref, route_ref, cnt_ref, wq_ref, wk_ref, wv_ref, wo_ref, k_ref, v_ref,
                  q0_ref, q1_ref, a0_ref, a1_ref):
    T = q0_ref.shape[0]

    @pl.when(jnp.logical_and(pl.program_id(0) == 0, pl.program_id(1) == 0))
    def _():
        cnt_ref[...] = jnp.zeros_like(cnt_ref)
        for w32_ref, w_ref in ((wq32_ref, wq_ref), (wk32_ref, wk_ref), (wv32_ref, wv_ref), (wo32_ref, wo_ref)):
            for r in range(0, w_ref.shape[0], LANES):
                w_ref[r:r + LANES, :] = w32_ref[r:r + LANES, :].astype(BF16)

    @pl.when(pl.program_id(1) == 0)
    def _():
        mb = mem_ref[0].astype(BF16)
        k_ref[...] = jnp.dot(mb, wk_ref[...], preferred_element_type=F32).astype(BF16)
        v_ref[...] = jnp.dot(mb, wv_ref[...], preferred_element_type=F32).astype(BF16)

    bucket_ids = lax.broadcasted_iota(jnp.int32, (cnt_ref.shape[0], 1), 0).astype(F32)
    for half, (q_ref, a_ref) in enumerate(((q0_ref, a0_ref), (q1_ref, a1_ref))):
        rows = pl.ds(half * T, T)
        q = jnp.dot(x_ref[0, rows, :].astype(BF16), wq_ref[...], preferred_element_type=F32)
        q_ref[...] = (q * (XATTN_DH ** -0.5)).astype(BF16)
        for h in range(XATTN_HEADS):
            cols = slice(h * XATTN_DH, (h + 1) * XATTN_DH)
            s = lax.dot_general(q_ref[:, cols], k_ref[:, cols], (((1,), (1,)), ((), ())),
                                preferred_element_type=F32)
            e = jnp.exp(s - jnp.max(s, axis=-1, keepdims=True))
            p = e / jnp.sum(e, axis=-1, keepdims=True)
            a_ref[:, cols] = jnp.dot(p.astype(BF16), v_ref[:, cols],
                                     preferred_element_type=F32).astype(BF16)
        hres = jnp.dot(a_ref[...], wo_ref[...], preferred_element_type=F32)
        x2 = _layer_norm(DN_ALPHA * x_ref[0, rows, :] + hres, lng_ref[...], lnb_ref[...])
        o_ref[0, rows, 0:D_MODEL] = x2

        lt = lax.dot_general(wr_ref[...], x2.astype(BF16), (((1,), (1,)), ((), ())),
                             preferred_element_type=F32) + br_ref[...]
        bucket, g_lo, g_hi = _route([lt[i:i + 1, :] for i in range(N_EXPERTS)])
        route_ref[0, 0:1, rows] = bucket
        cnt_ref[...] += jnp.sum((bucket == bucket_ids).astype(F32), axis=1, keepdims=True)
        rt = jnp.concatenate([bucket, g_lo, g_hi, jnp.zeros((LANES - 3, T), F32)], axis=0)
        o_ref[0, rows, D_MODEL:D_EXT] = rt.T


def _xattn(x, mem, layer, wq, wk, wv, wo, lng, lnb, wr_t, br_col):
    B, S, D = x.shape
    T = TOK_TILE
    nt = S // (2 * T)
    tile = pl.BlockSpec((1, 2 * T, D), lambda b, s: (b, s, 0))
    proj = pl.BlockSpec((None, D, D), lambda b, s: (layer, 0, 0), pipeline_mode=pl.Buffered(1))
    consts = (lng, lnb, wr_t, br_col)
    return pl.pallas_call(
        _xattn_kernel,
        grid=(B, nt),
        in_specs=[tile, pl.BlockSpec((1, N_MEM, D), lambda b, s: (b, 0, 0))] + [proj] * 4
        + [_const_spec(c.shape, single_buffer=True) for c in consts],
        out_specs=[pl.BlockSpec((1, 2 * T, D_EXT), lambda b, s: (b, s, 0)),
                   pl.BlockSpec((1, 1, 2 * T), lambda b, s: (b * nt + s, 0, 0)),
                   pl.BlockSpec((CNT_ROWS, LANES), lambda b, s: (0, 0))],
        out_shape=[jax.ShapeDtypeStruct((B, S, D_EXT), F32),
                   jax.ShapeDtypeStruct((B * nt, 1, 2 * T), F32),
                   jax.ShapeDtypeStruct((CNT_ROWS, LANES), F32)],
        scratch_shapes=[
            pltpu.VMEM((D, D), BF16),
            pltpu.VMEM((D, D), BF16),
            pltpu.VMEM((D, D), BF16),
            pltpu.VMEM((D, D), BF16),
            pltpu.VMEM((N_MEM, D), BF16),
            pltpu.VMEM((N_MEM, D), BF16),
            pltpu.VMEM((T, D), BF16),
            pltpu.VMEM((T, D), BF16),
            pltpu.VMEM((T, D), BF16),
            pltpu.VMEM((T, D), BF16),
        ],
        compiler_params=pltpu.CompilerParams(
            dimension_semantics=("arbitrary", "arbitrary"), vmem_limit_bytes=VMEM_LIMIT),
        name="xattn",
    )(x, mem, wq, wk, wv, wo, *consts)


def _moe_kernel(elo_ref, ehi_ref, nrows_ref, start_ref, order_ref, x_hbm,
                wg_lo, wu_lo, wd_lo, wg_hi, wu_hi, wd_hi, lng_ref, lnb_ref, o_hbm,
                xbuf, obuf, gsem, ssem):
    del elo_ref, ehi_ref
    i = pl.program_id(0)
    n_tiles = pl.num_programs(0)
    tm = xbuf.shape[0] * SUBLANES
    slot = lax.rem(i, 2)
    nrows = nrows_ref[i]
    nxt = jnp.minimum(i + 1, n_tiles - 1)
    has_next = jnp.logical_and(i + 1 < n_tiles, nrows_ref[nxt] > 0)

    def gather_row(base, j, k):
        t = order_ref[base + j * SUBLANES + k]
        pltpu.make_async_copy(x_hbm.at[pl.ds(t, 1)], xbuf.at[j, pl.ds(k, 1)], gsem.at[0]).start(k % 2)

    def scatter_row(base, s, j, k):
        t = order_ref[base + j * SUBLANES + k]
        pltpu.make_async_copy(obuf.at[s, j, pl.ds(k, 1)], o_hbm.at[pl.ds(t, 1)], ssem.at[s]).start(k % 2)

    def wait_gather():
        pltpu.make_async_copy(xbuf, xbuf, gsem.at[0]).wait()

    def wait_scatter(s, n):
        @pl.when(n == tm)
        def _():
            pltpu.make_async_copy(obuf.at[s], obuf.at[s], ssem.at[s]).wait()

        @pl.when(n < tm)
        def _():
            def body(r, c):
                pltpu.make_async_copy(obuf.at[s, 0, pl.ds(0, 1)], obuf.at[s, 0, pl.ds(0, 1)],
                                      ssem.at[s]).wait()
                return c
            lax.fori_loop(0, n, body, 0)

    @pl.when(jnp.logical_and(i == 0, nrows > 0))
    def _():
        base0 = start_ref[0]

        def body(j, c):
            for k in range(SUBLANES):
                gather_row(base0, j, k)
            return c
        lax.fori_loop(0, tm // SUBLANES, body, 0)

    @pl.when(nrows > 0)
    def _():
        wait_gather()
        xe = xbuf[...].reshape(tm, xbuf.shape[2])
        x = xe[:, 0:D_MODEL]
        g_lo = xe[:, COL_GLO:COL_GLO + 1]
        g_hi = xe[:, COL_GHI:COL_GHI + 1]
        xb = x.astype(BF16)

        def expert(wg, wu, wd):
            half = D_EXPERT // 2
            acc = None
            for c in range(2):
                cols = slice(c * half, (c + 1) * half)
                hg = jnp.dot(xb, wg[0, :, cols], preferred_element_type=F32)
                hu = jnp.dot(xb, wu[0, :, cols], preferred_element_type=F32)
                part = jnp.dot((_silu(hg) * hu).astype(BF16), wd[0, cols, :], preferred_element_type=F32)
                acc = part if acc is None else acc + part
            return acc

        y = g_lo * expert(wg_lo, wu_lo, wd_lo)
        y = y + g_hi * expert(wg_hi, wu_hi, wd_hi)
        res = _layer_norm(DN_ALPHA * x + y, lng_ref[...], lnb_ref[...])
        base_n = start_ref[nxt]
        for r in range(tm):
            gather_row(base_n, r // SUBLANES, r % SUBLANES)
        obuf[slot] = res.reshape(tm // SUBLANES, SUBLANES, D_MODEL)

        base = start_ref[i]
        for s in range(2):
            @pl.when(jnp.logical_and(nrows == tm, slot == s))
            def _(s=s):
                for r in range(tm):
                    scatter_row(base, s, r // SUBLANES, r % SUBLANES)

        @pl.when(nrows < tm)
        def _():
            def body(r, c):
                t = order_ref[base + r]
                pltpu.make_async_copy(obuf.at[slot, r // SUBLANES, pl.ds(r % SUBLANES, 1)],
                                      o_hbm.at[pl.ds(t, 1)], ssem.at[slot]).start()
                return c
            lax.fori_loop(0, nrows, body, 0)

        @pl.when(i > 0)
        def _():
            wait_scatter(1 - slot, nrows_ref[jnp.maximum(i - 1, 0)])

        @pl.when(jnp.logical_not(has_next))
        def _():
            wait_scatter(slot, nrows)
            wait_gather()


def _moe(x2e, order_pad, layer, wg, wu, wd, lng, lnb, tile_elo, tile_ehi, tile_nrows, tile_start):
    n_tok, de = x2e.shape
    n_tiles = tile_start.shape[0]
    tm, D = MOE_TILE, D_MODEL
    wlo = lambda s: pl.BlockSpec((None, 1) + s, lambda i, elo, ehi, nr, st, od: (layer, elo[i], 0, 0))
    whi = lambda s: pl.BlockSpec((None, 1) + s, lambda i, elo, ehi, nr, st, od: (layer, ehi[i], 0, 0))
    cst = lambda s: pl.BlockSpec(s, lambda i, elo, ehi, nr, st, od: (0, 0))
    gu, dn = (D, D_EXPERT), (D_EXPERT, D)
    grid_spec = pltpu.PrefetchScalarGridSpec(
        num_scalar_prefetch=5,
        grid=(n_tiles,),
        in_specs=[pl.BlockSpec(memory_space=pl.ANY),
                  wlo(gu), wlo(gu), wlo(dn), whi(gu), whi(gu), whi(dn),
                  cst(lng.shape), cst(lnb.shape)],
        out_specs=pl.BlockSpec(memory_space=pl.ANY),
        scratch_shapes=[
            pltpu.VMEM((tm // SUBLANES, SUBLANES, de), F32),
            pltpu.VMEM((2, tm // SUBLANES, SUBLANES, D), F32),
            pltpu.SemaphoreType.DMA((1,)),
            pltpu.SemaphoreType.DMA((2,)),
        ],
    )
    return pl.pallas_call(
        _moe_kernel,
        grid_spec=grid_spec,
        out_shape=jax.ShapeDtypeStruct((n_tok, D), F32),
        compiler_params=pltpu.CompilerParams(
            dimension_semantics=("arbitrary",), vmem_limit_bytes=VMEM_LIMIT),
        name="moe",
    )(tile_elo, tile_ehi, tile_nrows, tile_start, order_pad, x2e, wg, wu, wd, wg, wu, wd, lng, lnb)


def _dispatch_plan(bucket, counts, n_tok):
    tm = MOE_TILE
    n_tiles = n_tok // tm + N_BUCKETS
    assert n_tok <= 1 << 16
    keys = bucket * (1 << 16) + jnp.arange(n_tok, dtype=jnp.int32)
    order = lax.sort(keys, is_stable=False) & ((1 << 16) - 1)
    tiles_per = (counts + tm - 1) // tm
    tile_end = jnp.cumsum(tiles_per)
    tile_start = tile_end - tiles_per
    sorted_start = jnp.cumsum(counts) - counts
    n_valid = tile_end[-1]
    ti = jnp.arange(n_tiles, dtype=jnp.int32)
    valid = ti < n_valid
    in_bucket = lambda t: jnp.logical_and(t >= tile_start[None, :], t < tile_end[None, :]).astype(jnp.int32)
    member = in_bucket(ti[:, None])
    last_member = in_bucket(jnp.reshape(n_valid - 1, (1, 1)))
    pick = lambda table, m: jnp.sum(m * table[None, :], axis=1)
    e_lo = jnp.asarray([b // len(PAIRS) * EXPERTS_PER_GROUP + PAIRS[b % len(PAIRS)][0]
                        for b in range(N_BUCKETS)], jnp.int32)
    e_hi = jnp.asarray([b // len(PAIRS) * EXPERTS_PER_GROUP + PAIRS[b % len(PAIRS)][1]
                        for b in range(N_BUCKETS)], jnp.int32)
    tile_elo = jnp.where(valid, pick(e_lo, member), pick(e_lo, last_member))
    tile_ehi = jnp.where(valid, pick(e_hi, member), pick(e_hi, last_member))
    first_row = (ti - pick(tile_start, member)) * tm
    tile_nrows = jnp.where(valid, jnp.clip(pick(counts, member) - first_row, 0, tm), 0)
    start = jnp.where(valid, jnp.clip(pick(sorted_start, member) + first_row, 0, n_tok), n_tok)
    order_pad = jnp.concatenate([order, jnp.zeros((tm,), jnp.int32)])
    return (order_pad, tile_elo.astype(jnp.int32), tile_ehi.astype(jnp.int32),
            tile_nrows.astype(jnp.int32), start.astype(jnp.int32))


def kernel(x, mem, w_in, w_a2, b_a, gla_norm_g, w_s, b_s, sgu_ln_g, sgu_ln_b, w_out, wq_x, wk_x, wv_x,
           wo_x, w_router, b_router, w_gate, w_up, w_down, ln_g, ln_b):
    B, S, D = x.shape
    n_tok = B * S

    segs = (slice(0, 256), slice(256, 512), slice(512, 1024), slice(1024, 1536),
            slice(1552, 2064), slice(2064, 2576))
    w_in_r = jnp.concatenate(
        [w_in[:, :, s] for s in segs]
        + [jnp.pad(w_in[:, :, 1536:1552], ((0, 0), (0, 0), (0, A_PAD - GATE_RANK)))], axis=-1).astype(BF16)
    w_a2_p = jnp.pad(w_a2, ((0, 0), (0, A_PAD - GATE_RANK), (0, 0))).astype(BF16)
    bs_full = jnp.broadcast_to(b_s[:, :, :, None], b_s.shape + (SGU_CH,))
    w_out_b = w_out.astype(BF16)
    wg_b, wu_b, wd_b = (w.astype(BF16) for w in (w_gate, w_up, w_down))
    wr_t = w_router.T.astype(BF16)
    br_col = jnp.broadcast_to(b_router[:, None], (N_EXPERTS, TOK_TILE))
    row = lambda v: v.reshape(1, -1)

    for l in range(DEPTH):
        x = _mixer(x, l, w_in_r, w_a2_p, row(b_a[l]), row(gla_norm_g[l]), w_s, bs_full,
                   row(sgu_ln_g[l]), row(sgu_ln_b[l]), w_out_b, row(ln_g[l, 0]), row(ln_b[l, 0]))
        x2e, route, cnt = _xattn(x, mem, l, wq_x, wk_x, wv_x, wo_x, row(ln_g[l, 1]), row(ln_b[l, 1]),
                                 wr_t, br_col)
        bucket = route.reshape(-1).astype(jnp.int32)
        counts = cnt[:N_BUCKETS, 0].astype(jnp.int32)
        order_pad, t_elo, t_ehi, t_nrows, t_start = _dispatch_plan(bucket, counts, n_tok)
        x = _moe(x2e.reshape(n_tok, D_EXT), order_pad, l, wg_b, wu_b, wd_b,
                 row(ln_g[l, 2]), row(ln_b[l, 2]), t_elo, t_ehi, t_nrows, t_start).reshape(B, S, D)
    return x
```
